```python
import math
import jax, jax.numpy as jnp
from jax import lax
import numpy as np

D_MODEL = 4096
BATCH = 2
SEQ = 4096
DEPTH = 2

N_BRANCH = 3
CHUNK = 128
HD = 128
D_A = D_MODEL // 2
H_A = D_A // HD
D_B = D_MODEL // 2
SC_K = 3
D_C = D_MODEL // 2
CF_K = 31
D_FF = 11008
FF_K = 3
ALPHA = (2.0 * DEPTH) ** 0.25
BETA = (8.0 * DEPTH) ** -0.25
LN_EPS = 1e-5

OFF_A = 0
OFF_B = OFF_A + 2 * D_A
OFF_C = OFF_B + 3 * D_B
OFF_G = OFF_C + 2 * D_C
N_IN = OFF_G + N_BRANCH * D_MODEL

kernel_name = "hybrid_gated_sgu_shortconv_conformer_deepnorm"


def layer_norm(x, g, b):
    xf = x.astype(jnp.float32)
    mu = jnp.mean(xf, axis=-1, keepdims=True)
    var = jnp.mean(jnp.square(xf - mu), axis=-1, keepdims=True)
    y = (xf - mu) * lax.rsqrt(var + LN_EPS)
    return (y * g.astype(jnp.float32) + b.astype(jnp.float32)).astype(x.dtype)


def causal_dwconv(x, w, b=None):
    k = w.shape[0]
    y = lax.conv_general_dilated(
        x, w[:, None, :].astype(x.dtype), window_strides=(1,), padding=[(k - 1, 0)],
        dimension_numbers=("NWC", "WIO", "NWC"), feature_group_count=x.shape[-1])
    return y if b is None else y + b


def chunked_sgu(z, ln_g, ln_b, ws, bs):
    bsz, s, _ = z.shape
    n = s // CHUNK
    u, v = z[..., :D_A], z[..., D_A:]
    v = layer_norm(v.reshape(bsz, s, H_A, HD), ln_g.reshape(H_A, HD), ln_b.reshape(H_A, HD))
    v = v.reshape(bsz, n, CHUNK, H_A, HD)
    mask = jnp.tril(jnp.ones((CHUNK, CHUNK), dtype=bool))
    w = jnp.where(mask[None], ws, 0).astype(v.dtype)
    mixed = jnp.einsum("hts,bnshc->bnthc", w, v) + jnp.swapaxes(bs, 0, 1)[:, :, None]
    return (u.reshape(bsz, n, CHUNK, H_A, HD) * mixed).reshape(bsz, s, D_A)


def hybrid_mixer(x, w_in, b_gate, a_ln_g, a_ln_b, a_ws, a_bs, a_out, b_conv, b_out,
                 c_conv, c_conv_b, c_ln_g, c_ln_b, c_out, w_o):
    bsz, s, d = x.shape
    z = jnp.einsum("bsd,dn->bsn", x, w_in)
    za = jax.nn.gelu(z[..., OFF_A:OFF_B])
    y_a = jnp.einsum("bsc,cd->bsd", chunked_sgu(za, a_ln_g, a_ln_b, a_ws, a_bs), a_out)
    gb = z[..., OFF_B:OFF_B + D_B]
    gc = z[..., OFF_B + D_B:OFF_B + 2 * D_B]
    hb = z[..., OFF_B + 2 * D_B:OFF_C]
    y_b = jnp.einsum("bsc,cd->bsd", gb * causal_dwconv(gc * hb, b_conv), b_out)
    ca = z[..., OFF_C:OFF_C + D_C] * jax.nn.sigmoid(z[..., OFF_C + D_C:OFF_G])
    cc = jax.nn.silu(layer_norm(causal_dwconv(ca, c_conv, c_conv_b), c_ln_g, c_ln_b))
    y_c = jnp.einsum("bsc,cd->bsd", cc, c_out)
    g = jax.nn.sigmoid(z[..., OFF_G:].reshape(bsz, s, N_BRANCH, d) + b_gate)
    m = g[:, :, 0] * y_a + g[:, :, 1] * y_b + g[:, :, 2] * y_c
    return jnp.einsum("bsd,de->bse", m, w_o)


def conv_ffn(x, f_up, f_conv, f_down):
    h = causal_dwconv(jnp.einsum("bsd,df->bsf", x, f_up), f_conv)
    return jnp.einsum("bsf,fd->bsd", jax.nn.silu(h[..., :D_FF]) * h[..., D_FF:], f_down)


def setup_inputs(seed: int = 0) -> dict:
    key = jax.random.key(seed)
    ks = jax.random.split(key, 24)
    L, D = DEPTH, D_MODEL

    def nrm(k, shape, scale):
        return jax.random.normal(k, shape, jnp.float32) * scale

    return {
        "x": nrm(ks[0], (BATCH, SEQ, D), 1.0),
        "w_in": nrm(ks[1], (L, D, N_IN), D ** -0.5),
        "b_gate": nrm(ks[2], (L, N_BRANCH, D), 0.02),
        "a_ln_g": 1.0 + nrm(ks[3], (L, D_A), 0.02),
        "a_ln_b": nrm(ks[4], (L, D_A), 0.02),
        "a_ws": nrm(ks[5], (L, H_A, CHUNK, CHUNK), CHUNK ** -0.5),
        "a_bs": 1.0 + nrm(ks[6], (L, H_A, CHUNK), 0.02),
        "a_out": nrm(ks[7], (L, D_A, D), D_A ** -0.5 * BETA),
        "b_conv": nrm(ks[8], (L, SC_K, D_B), SC_K ** -0.5),
        "b_out": nrm(ks[9], (L, D_B, D), D_B ** -0.5 * BETA),
        "c_conv": nrm(ks[10], (L, CF_K, D_C), CF_K ** -0.5),
        "c_conv_b": nrm(ks[11], (L, D_C), 0.02),
        "c_ln_g": 1.0 + nrm(ks[12], (L, D_C), 0.02),
        "c_ln_b": nrm(ks[13], (L, D_C), 0.02),
        "c_out": nrm(ks[14], (L, D_C, D), D_C ** -0.5 * BETA),
        "w_o": nrm(ks[15], (L, D, D), D ** -0.5 * BETA),
        "ln1_g": 1.0 + nrm(ks[16], (L, D), 0.02),
        "ln1_b": nrm(ks[17], (L, D), 0.02),
        "f_up": nrm(ks[18], (L, D, 2 * D_FF), D ** -0.5 * BETA),
        "f_conv": nrm(ks[19], (L, FF_K, 2 * D_FF), FF_K ** -0.5),
        "f_down": nrm(ks[20], (L, D_FF, D), D_FF ** -0.5 * BETA),
        "ln2_g": 1.0 + nrm(ks[21], (L, D), 0.02),
        "ln2_b": nrm(ks[22], (L, D), 0.02),
    }


def reference(x, w_in, b_gate, a_ln_g, a_ln_b, a_ws, a_bs, a_out, b_conv, b_out,
              c_conv, c_conv_b, c_ln_g, c_ln_b, c_out, w_o, ln1_g, ln1_b,
              f_up, f_conv, f_down, ln2_g, ln2_b):
    for l in range(DEPTH):
        mix = hybrid_mixer(x, w_in[l], b_gate[l], a_ln_g[l], a_ln_b[l], a_ws[l], a_bs[l], a_out[l],
                           b_conv[l], b_out[l], c_conv[l], c_conv_b[l], c_ln_g[l], c_ln_b[l],
                           c_out[l], w_o[l])
        x = layer_norm(ALPHA * x + mix, ln1_g[l], ln1_b[l])
        x = layer_norm(ALPHA * x + conv_ffn(x, f_up[l], f_conv[l], f_down[l]), ln2_g[l], ln2_b[l])
    return x
```

```python
import functools
import math

import jax
import jax.numpy as jnp
from jax.experimental import pallas as pl
from jax.experimental.pallas import tpu as pltpu

F32 = jnp.float32
BF16 = jnp.bfloat16

D_MODEL = 4096
CHUNK = 128
HD = 128
D_A = D_MODEL // 2
H_A = D_A // HD
D_B = D_MODEL // 2
SC_K = 3
D_C = D_MODEL // 2
CF_K = 31
D_FF = 11008
FF_K = 3
N_BRANCH = 3
LN_EPS = 1e-5
OFF_A = 0
OFF_B = OFF_A + 2 * D_A
OFF_C = OFF_B + 3 * D_B
OFF_G = OFF_C + 2 * D_C
N_IN = OFF_G + N_BRANCH * D_MODEL

V7X_SUBLANES = 8
V7X_LANES = 128
V7X_VMEM_BYTES = 64 << 20

TM = 1024
TN_FRONT = 256
TN_MERGE = 256
TN_OUT = 512
TF = 256
DOWN_COLS = 512
TR_LN = 256
CONV_ROWS = 32
HALO_SHORT = V7X_SUBLANES
HALO_LONG = 32
VMEM_LIMIT = V7X_VMEM_BYTES - (8 << 20)


def _sigmoid(v):
    return 1.0 / (1.0 + jnp.exp(-v))


def _gelu_tanh(v):
    c = math.sqrt(2.0 / math.pi)
    return 0.5 * v * (1.0 + jnp.tanh(c * (v + 0.044715 * (v * v * v))))


def _dot(a, b):
    return jnp.dot(a, b, preferred_element_type=F32)


def _params(*sem):
    return pltpu.CompilerParams(dimension_semantics=sem, vmem_limit_bytes=VMEM_LIMIT)


def _sgu_kernel(x_ref, wu_ref, wv_ref, g_ref, b_ref, ws_ref, bs_ref, o_ref):
    tm, tn = o_ref.shape
    x = x_ref[...]
    u = _gelu_tanh(_dot(x, wu_ref[...]))
    v = _gelu_tanh(_dot(x, wv_ref[...]))
    row = jax.lax.broadcasted_iota(jnp.int32, (CHUNK, CHUNK), 0)
    col = jax.lax.broadcasted_iota(jnp.int32, (CHUNK, CHUNK), 1)
    causal = row >= col
    for h in range(tn // HD):
        hs = slice(h * HD, (h + 1) * HD)
        vh = v[:, hs]
        mu = jnp.mean(vh, axis=-1, keepdims=True)
        dv = vh - mu
        var = jnp.mean(dv * dv, axis=-1, keepdims=True)
        vn = (dv * jax.lax.rsqrt(var + LN_EPS) * g_ref[:, hs] + b_ref[:, hs]).astype(BF16)
        w = jnp.where(causal, ws_ref[h], 0.0).astype(BF16)
        bs = bs_ref[h]
        for c in range(tm // CHUNK):
            rs = slice(c * CHUNK, (c + 1) * CHUNK)
            mixed = _dot(w, vn[rs, :]) + bs
            o_ref[rs, hs] = (u[rs, hs] * mixed).astype(o_ref.dtype)


def _sgu_branch(xb, w_in, ln_g, ln_b, ws, bs_wide, layer):
    t = xb.shape[0]
    tn = TN_FRONT
    nj = D_A // tn
    return pl.pallas_call(
        _sgu_kernel,
        grid=(t // TM, nj),
        in_specs=[
            pl.BlockSpec((TM, D_MODEL), lambda i, j: (i, 0)),
            pl.BlockSpec((None, D_MODEL, tn), lambda i, j: (layer, 0, OFF_A // tn + j)),
            pl.BlockSpec((None, D_MODEL, tn), lambda i, j: (layer, 0, (OFF_A + D_A) // tn + j)),
            pl.BlockSpec((None, 1, tn), lambda i, j: (layer, 0, j)),
            pl.BlockSpec((None, 1, tn), lambda i, j: (layer, 0, j)),
            pl.BlockSpec((None, tn // HD, CHUNK, CHUNK), lambda i, j: (layer, j, 0, 0)),
            pl.BlockSpec((None, tn // HD, CHUNK, HD), lambda i, j: (layer, j, 0, 0)),
        ],
        out_specs=pl.BlockSpec((TM, tn), lambda i, j: (i, j)),
        out_shape=jax.ShapeDtypeStruct((t, D_A), BF16),
        compiler_params=_params("parallel", "arbitrary"),
        name="sgu_branch",
    )(xb, w_in, w_in, ln_g, ln_b, ws, bs_wide)


def _stage_rows(buf_ref, carry_ref, j, vals, halo, seq_start):
    tm = vals.shape[0]

    @pl.when(seq_start)
    def _():
        buf_ref[0:halo, :] = jnp.zeros((halo, vals.shape[1]), F32)

    @pl.when(jnp.logical_not(seq_start))
    def _():
        buf_ref[0:halo, :] = carry_ref[j]

    buf_ref[halo:halo + tm, :] = vals
    carry_ref[j] = vals[tm - halo:, :]


def _conv3(buf_ref, vals, cw, halo):
    tm = vals.shape[0]
    return (cw[0:1, :] * buf_ref[halo - 2:halo - 2 + tm, :]
            + cw[1:2, :] * buf_ref[halo - 1:halo - 1 + tm, :]
            + cw[2:3, :] * vals)


def _sconv_kernel(x_ref, wb_ref, wc_ref, wh_ref, cw_ref, o_ref, carry_ref, buf_ref, *, blocks_per_seq):
    i, j = pl.program_id(0), pl.program_id(1)
    x = x_ref[...]
    p = _dot(x, wc_ref[...]) * _dot(x, wh_ref[...])
    _stage_rows(buf_ref, carry_ref, j, p, HALO_SHORT, i % blocks_per_seq == 0)
    y = _conv3(buf_ref, p, cw_ref[...], HALO_SHORT)
    o_ref[...] = (_dot(x, wb_ref[...]) * y).astype(o_ref.dtype)


def _sconv_branch(xb, w_in, b_conv, layer, blocks_per_seq):
    t = xb.shape[0]
    tn = TN_FRONT
    nj = D_B // tn
    wspec = lambda off: pl.BlockSpec((None, D_MODEL, tn), lambda i, j: (layer, 0, off // tn + j))
    return pl.pallas_call(
        functools.partial(_sconv_kernel, blocks_per_seq=blocks_per_seq),
        grid=(t // TM, nj),
        in_specs=[
            pl.BlockSpec((TM, D_MODEL), lambda i, j: (i, 0)),
            wspec(OFF_B), wspec(OFF_B + D_B), wspec(OFF_B + 2 * D_B),
            pl.BlockSpec((None, SC_K, tn), lambda i, j: (layer, 0, j)),
        ],
        out_specs=pl.BlockSpec((TM, tn), lambda i, j: (i, j)),
        out_shape=jax.ShapeDtypeStruct((t, D_B), BF16),
        scratch_shapes=[pltpu.VMEM((nj, HALO_SHORT, tn), F32),
                        pltpu.VMEM((HALO_SHORT + TM, tn), F32)],
        compiler_params=_params("arbitrary", "arbitrary"),
        name="sconv_branch",
    )(xb, w_in, w_in, w_in, b_conv)


def _cconv_kernel(x_ref, wa_ref, wg_ref, cw_ref, cb_ref, o_ref, carry_ref, buf_ref, *, blocks_per_seq):
    i, j = pl.program_id(0), pl.program_id(1)
    tm, tn = o_ref.shape
    x = x_ref[...]
    ca = _dot(x, wa_ref[...]) * _sigmoid(_dot(x, wg_ref[...]))
    _stage_rows(buf_ref, carry_ref, j, ca, HALO_LONG, i % blocks_per_seq == 0)
    cw = cw_ref[...]
    bias = jnp.broadcast_to(cb_ref[...], (CONV_ROWS, tn))
    first = HALO_LONG - (CF_K - 1)
    for r in range(tm // CONV_ROWS):
        base = r * CONV_ROWS
        acc = bias
        for k in range(CF_K):
            acc = acc + cw[k:k + 1, :] * buf_ref[base + first + k:base + first + k + CONV_ROWS, :]
        o_ref[base:base + CONV_ROWS, :] = acc


def _cconv_branch(xb, w_in, c_conv, c_conv_b, layer, blocks_per_seq):
    t = xb.shape[0]
    tn = TN_FRONT
    nj = D_C // tn
    wspec = lambda off: pl.BlockSpec((None, D_MODEL, tn), lambda i, j: (layer, 0, off // tn + j))
    return pl.pallas_call(
        functools.partial(_cconv_kernel, blocks_per_seq=blocks_per_seq),
        grid=(t // TM, nj),
        in_specs=[
            pl.BlockSpec((TM, D_MODEL), lambda i, j: (i, 0)),
            wspec(OFF_C), wspec(OFF_C + D_C),
            pl.BlockSpec((None, CF_K, tn), lambda i, j: (layer, 0, j)),
            pl.BlockSpec((None, 1, tn), lambda i, j: (layer, 0, j)),
        ],
        out_specs=pl.BlockSpec((TM, tn), lambda i, j: (i, j)),
        out_shape=jax.ShapeDtypeStruct((t, D_C), F32),
        scratch_shapes=[pltpu.VMEM((nj, HALO_LONG, tn), F32),
                        pltpu.VMEM((HALO_LONG + TM, tn), F32)],
        compiler_params=_params("arbitrary", "arbitrary"),
        name="cconv_branch",
    )(xb, w_in, w_in, c_conv, c_conv_b)


def _layer_norm_rows(v, g, b):
    mu = jnp.mean(v, axis=-1, keepdims=True)
    dv = v - mu
    var = jnp.mean(dv * dv, axis=-1, keepdims=True)
    return dv * jax.lax.rsqrt(var + LN_EPS) * g + b


def _ln_silu_kernel(v_ref, g_ref, b_ref, o_ref):
    y = _layer_norm_rows(v_ref[...], g_ref[...], b_ref[...])
    o_ref[...] = (y * _sigmoid(y)).astype(o_ref.dtype)


def _ln_silu(v, g, b, layer):
    t, c = v.shape
    return pl.pallas_call(
        _ln_silu_kernel,
        grid=(t // TR_LN,),
        in_specs=[pl.BlockSpec((TR_LN, c), lambda i: (i, 0)),
                  pl.BlockSpec((None, 1, c), lambda i: (layer, 0, 0)),
                  pl.BlockSpec((None, 1, c), lambda i: (layer, 0, 0))],
        out_specs=pl.BlockSpec((TR_LN, c), lambda i: (i, 0)),
        out_shape=jax.ShapeDtypeStruct((t, c), BF16),
        compiler_params=_params("parallel"),
        name="ln_silu",
    )(v, g, b)


def _residual_ln_kernel(x_ref, y_ref, g_ref, b_ref, o_ref, ob_ref, *, alpha):
    y = _layer_norm_rows(alpha * x_ref[...] + y_ref[...], g_ref[...], b_ref[...])
    o_ref[...] = y
    ob_ref[...] = y.astype(ob_ref.dtype)


def _residual_ln(x, y, g, b, layer, alpha):
    t, c = x.shape
    row_spec = pl.BlockSpec((TR_LN, c), lambda i: (i, 0))
    par_spec = pl.BlockSpec((None, 1, c), lambda i: (layer, 0, 0))
    return pl.pallas_call(
        functools.partial(_residual_ln_kernel, alpha=alpha),
        grid=(t // TR_LN,),
        in_specs=[row_spec, row_spec, par_spec, par_spec],
        out_specs=[row_spec, row_spec],
        out_shape=[jax.ShapeDtypeStruct((t, c), F32), jax.ShapeDtypeStruct((t, c), BF16)],
        compiler_params=_params("parallel"),
        name="residual_ln",
    )(x, y, g, b)


def _merge_kernel(x_ref, a_ref, b_ref, c_ref, wg0_ref, wg1_ref, wg2_ref, wa_ref, wb_ref, wc_ref,
                  bg_ref, o_ref):
    x = x_ref[...]
    g0 = _sigmoid(_dot(x, wg0_ref[...]) + bg_ref[0:1, :])
    m = g0 * _dot(a_ref[...], wa_ref[...])
    g1 = _sigmoid(_dot(x, wg1_ref[...]) + bg_ref[1:2, :])
    m = m + g1 * _dot(b_ref[...], wb_ref[...])
    g2 = _sigmoid(_dot(x, wg2_ref[...]) + bg_ref[2:3, :])
    m = m + g2 * _dot(c_ref[...], wc_ref[...])
    o_ref[...] = m.astype(o_ref.dtype)


def _merge(xb, act_a, act_b, act_c, w_in, a_out, b_out, c_out, b_gate, layer):
    t = xb.shape[0]
    tn = TN_MERGE
    resident = dict(pipeline_mode=pl.Buffered(1))
    gspec = lambda k: pl.BlockSpec((None, D_MODEL, tn),
                                   lambda i, j: (layer, 0, (OFF_G + k * D_MODEL) // tn + j))
    ospec = lambda width: pl.BlockSpec((None, width, tn), lambda i, j: (layer, 0, j))
    return pl.pallas_call(
        _merge_kernel,
        grid=(t // TM, D_MODEL // tn),
        in_specs=[
            pl.BlockSpec((TM, D_MODEL), lambda i, j: (i, 0), **resident),
            pl.BlockSpec((TM, D_A), lambda i, j: (i, 0), **resident),
            pl.BlockSpec((TM, D_B), lambda i, j: (i, 0), **resident),
            pl.BlockSpec((TM, D_C), lambda i, j: (i, 0), **resident),
            gspec(0), gspec(1), gspec(2),
            ospec(D_A), ospec(D_B), ospec(D_C),
            pl.BlockSpec((None, N_BRANCH, tn), lambda i, j: (layer, 0, j)),
        ],
        out_specs=pl.BlockSpec((TM, tn), lambda i, j: (i, j)),
        out_shape=jax.ShapeDtypeStruct((t, D_MODEL), BF16),
        compiler_params=_params("parallel", "arbitrary"),
        name="gate_merge",
    )(xb, act_a, act_b, act_c, w_in, w_in, w_in, a_out, b_out, c_out, b_gate)


def _proj_kernel(m_ref, w_ref, o_ref):
    o_ref[...] = _dot(m_ref[...], w_ref[...])


def _out_proj(m, w_o, layer):
    t = m.shape[0]
    tn = TN_OUT
    return pl.pallas_call(
        _proj_kernel,
        grid=(t // TM, D_MODEL // tn),
        in_specs=[pl.BlockSpec((TM, D_MODEL), lambda i, j: (i, 0)),
                  pl.BlockSpec((None, D_MODEL, tn), lambda i, j: (layer, 0, j))],
        out_specs=pl.BlockSpec((TM, tn), lambda i, j: (i, j)),
        out_shape=jax.ShapeDtypeStruct((t, D_MODEL), F32),
        compiler_params=_params("parallel", "arbitrary"),
        name="out_proj",
    )(m, w_o)


def _ffn_kernel(x_ref, w1_ref, w2_ref, cw1_ref, cw2_ref, wd_ref, o_ref,
                carry1_ref, carry2_ref, buf1_ref, buf2_ref, *, blocks_per_seq):
    i, j = pl.program_id(0), pl.program_id(1)
    seq_start = i % blocks_per_seq == 0
    x = x_ref[...]
    h1 = _dot(x, w1_ref[...])
    h2 = _dot(x, w2_ref[...])
    _stage_rows(buf1_ref, carry1_ref, j, h1, HALO_SHORT, seq_start)
    _stage_rows(buf2_ref, carry2_ref, j, h2, HALO_SHORT, seq_start)
    c1 = _conv3(buf1_ref, h1, cw1_ref[...], HALO_SHORT)
    c2 = _conv3(buf2_ref, h2, cw2_ref[...], HALO_SHORT)
    act = (c1 * _sigmoid(c1) * c2).astype(BF16)

    @pl.when(j == 0)
    def _():
        o_ref[...] = jnp.zeros(o_ref.shape, F32)

    for c in range(D_MODEL // DOWN_COLS):
        cs = slice(c * DOWN_COLS, (c + 1) * DOWN_COLS)
        o_ref[:, cs] += _dot(act, wd_ref[:, cs])


def _conv_ffn(xb, f_up, f_conv, f_down, layer, blocks_per_seq):
    t = xb.shape[0]
    nj = D_FF // TF
    return pl.pallas_call(
        functools.partial(_ffn_kernel, blocks_per_seq=blocks_per_seq),
        grid=(t // TM, nj),
        in_specs=[
            pl.BlockSpec((TM, D_MODEL), lambda i, j: (i, 0), pipeline_mode=pl.Buffered(1)),
            pl.BlockSpec((None, D_MODEL, TF), lambda i, j: (layer, 0, j)),
            pl.BlockSpec((None, D_MODEL, TF), lambda i, j: (layer, 0, nj + j)),
            pl.BlockSpec((None, FF_K, TF), lambda i, j: (layer, 0, j)),
            pl.BlockSpec((None, FF_K, TF), lambda i, j: (layer, 0, nj + j)),
            pl.BlockSpec((None, TF, D_MODEL), lambda i, j: (layer, j, 0)),
        ],
        out_specs=pl.BlockSpec((TM, D_MODEL), lambda i, j: (i, 0), pipeline_mode=pl.Buffered(1)),
        out_shape=jax.ShapeDtypeStruct((t, D_MODEL), F32),
        scratch_shapes=[pltpu.VMEM((nj, HALO_SHORT, TF), F32),
                        pltpu.VMEM((nj, HALO_SHORT, TF), F32),
                        pltpu.VMEM((HALO_SHORT + TM, TF), F32),
                        pltpu.VMEM((HALO_SHORT + TM, TF), F32)],
        compiler_params=_params("arbitrary", "arbitrary"),
        name="conv_ffn",
    )(xb, f_up, f_up, f_conv, f_conv, f_down)


def kernel(x, w_in, b_gate, a_ln_g, a_ln_b, a_ws, a_bs, a_out, b_conv, b_out, c_conv, c_conv_b,
           c_ln_g, c_ln_b, c_out, w_o, ln1_g, ln1_b, f_up, f_conv, f_down, ln2_g, ln2_b):
    bsz, seq, d = x.shape
    depth = w_in.shape[0]
    assert d == D_MODEL and seq % TM == 0 and w_in.shape[-1] == N_IN and f_down.shape[1] == D_FF
    alpha = (2.0 * depth) ** 0.25
    blocks_per_seq = seq // TM
    t = bsz * seq

    as_bf16 = lambda w: w.astype(BF16)
    w_in_b, a_out_b, b_out_b, c_out_b, w_o_b, f_up_b, f_down_b = map(
        as_bf16, (w_in, a_out, b_out, c_out, w_o, f_up, f_down))
    row3 = lambda p: p[:, None, :]
    a_ln_g3, a_ln_b3, c_conv_b3, c_ln_g3, c_ln_b3 = map(row3, (a_ln_g, a_ln_b, c_conv_b, c_ln_g, c_ln_b))
    ln1_g3, ln1_b3, ln2_g3, ln2_b3 = map(row3, (ln1_g, ln1_b, ln2_g, ln2_b))
    bs_wide = jnp.broadcast_to(a_bs[..., None], a_bs.shape + (HD,))

    xf = x.reshape(t, d)
    xb = xf.astype(BF16)
    for l in range(depth):
        act_a = _sgu_branch(xb, w_in_b, a_ln_g3, a_ln_b3, a_ws, bs_wide, l)
        act_b = _sconv_branch(xb, w_in_b, b_conv, l, blocks_per_seq)
        pre_c = _cconv_branch(xb, w_in_b, c_conv, c_conv_b3, l, blocks_per_seq)
        act_c = _ln_silu(pre_c, c_ln_g3, c_ln_b3, l)
        m = _merge(xb, act_a, act_b, act_c, w_in_b, a_out_b, b_out_b, c_out_b, b_gate, l)
        mix = _out_proj(m, w_o_b, l)
        xf, xb = _residual_ln(xf, mix, ln1_g3, ln1_b3, l, alpha)
        ffn = _conv_ffn(xb, f_up_b, f_conv, f_down_b, l, blocks_per_seq)
        xf, xb = _residual_ln(xf, ffn, ln2_g3, ln2_b3, l, alpha)
    return xf.reshape(bsz, seq, d)
```

```python
import functools
import math

import jax
import jax.numpy as jnp
from jax.experimental import pallas as pl
from jax.experimental.pallas import tpu as pltpu

F32 = jnp.float32
BF16 = jnp.bfloat16

D_MODEL = 4096
CHUNK = 128
HD = 128
D_A = D_MODEL // 2
H_A = D_A // HD
D_B = D_MODEL // 2
SC_K = 3
D_C = D_MODEL // 2
CF_K = 31
D_FF = 11008
FF_K = 3
N_BRANCH = 3
LN_EPS = 1e-5
OFF_A = 0
OFF_B = OFF_A + 2 * D_A
OFF_C = OFF_B + 3 * D_B
OFF_G = OFF_C + 2 * D_C
N_IN = OFF_G + N_BRANCH * D_MODEL

V7X_SUBLANES = 8
V7X_LANES = 128
V7X_VMEM_BYTES = 64 << 20

TM = 1024
ROWS = 128
TN_FRONT = 256
TN_MERGE = 256
TN_OUT = 512
TF = 256
DOWN_COLS = 512
TR_LN = 256
HALO_SHORT = V7X_SUBLANES
HALO_LONG = 32
ROW_STRIDE = 4
ROW_GROUP = ROW_STRIDE * V7X_SUBLANES
VMEM_LIMIT = V7X_VMEM_BYTES - (8 << 20)

assert ROWS == CHUNK and TM % ROWS == 0 and ROWS % ROW_GROUP == 0


def _sigmoid(v):
    return 1.0 / (1.0 + jnp.exp(-v))


def _gelu_tanh(v):
    c = math.sqrt(2.0 / math.pi)
    return 0.5 * v * (1.0 + jnp.tanh(c * (v + 0.044715 * (v * v * v))))


def _dot(a, b):
    return jnp.dot(a, b, preferred_element_type=F32)


def _params(*sem):
    return pltpu.CompilerParams(dimension_semantics=sem, vmem_limit_bytes=VMEM_LIMIT)


def _stage_block(x_ref, xs_ref, first_use):
    @pl.when(first_use)
    def _():
        xs_ref[...] = x_ref[...]


def _init_halo(stage_halo, carry, seq_start):
    @pl.when(seq_start)
    def _():
        stage_halo[...] = jnp.zeros(stage_halo.shape, F32)

    @pl.when(jnp.logical_not(seq_start))
    def _():
        stage_halo[...] = carry[...]


def _conv3(stage, cw, halo, start, rows):
    at = lambda shift: stage[halo + start - shift:halo + start - shift + rows, :]
    return cw[0:1, :] * at(2) + cw[1:2, :] * at(1) + cw[2:3, :] * at(0)


def _sgu_kernel(x_ref, wu_ref, wv_ref, g_ref, b_ref, ws_ref, bs_ref, o_ref, xs_ref):
    tm, tn = o_ref.shape
    _stage_block(x_ref, xs_ref, pl.program_id(1) == 0)

    row = jax.lax.broadcasted_iota(jnp.int32, (CHUNK, CHUNK), 0)
    col = jax.lax.broadcasted_iota(jnp.int32, (CHUNK, CHUNK), 1)
    causal = row >= col
    heads = range(tn // HD)
    mix_w = [jnp.where(causal, ws_ref[h], 0.0).astype(BF16) for h in heads]

    def project(c):
        x = xs_ref[c * CHUNK:(c + 1) * CHUNK, :]
        return _dot(x, wu_ref[...]), _dot(x, wv_ref[...])

    def normalise(zv):
        v = _gelu_tanh(zv)
        out = []
        for h in heads:
            hs = slice(h * HD, (h + 1) * HD)
            vh = v[:, hs]
            mu = jnp.mean(vh, axis=-1, keepdims=True)
            dv = vh - mu
            var = jnp.mean(dv * dv, axis=-1, keepdims=True)
            out.append((dv * jax.lax.rsqrt(var + LN_EPS) * g_ref[:, hs] + b_ref[:, hs]).astype(BF16))
        return out

    def mix(c, zu, vn):
        u = _gelu_tanh(zu)
        for h in heads:
            hs = slice(h * HD, (h + 1) * HD)
            mixed = _dot(mix_w[h], vn[h]) + bs_ref[h]
            o_ref[c * CHUNK:(c + 1) * CHUNK, hs] = (u[:, hs] * mixed).astype(o_ref.dtype)

    zu, zv = project(0)
    vn = normalise(zv)
    for c in range(1, tm // CHUNK):
        zu_next, zv_next = project(c)
        mix(c - 1, zu, vn)
        vn = normalise(zv_next)
        zu = zu_next
    mix(tm // CHUNK - 1, zu, vn)


def _sgu_branch(xb, w_in, ln_g, ln_b, ws, bs_wide, layer):
    t = xb.shape[0]
    tn = TN_FRONT
    return pl.pallas_call(
        _sgu_kernel,
        grid=(t // TM, D_A // tn),
        in_specs=[
            pl.BlockSpec((TM, D_MODEL), lambda i, j: (i, 0)),
            pl.BlockSpec((None, D_MODEL, tn), lambda i, j: (layer, 0, OFF_A // tn + j)),
            pl.BlockSpec((None, D_MODEL, tn), lambda i, j: (layer, 0, (OFF_A + D_A) // tn + j)),
            pl.BlockSpec((None, 1, tn), lambda i, j: (layer, 0, j)),
            pl.BlockSpec((None, 1, tn), lambda i, j: (layer, 0, j)),
            pl.BlockSpec((None, tn // HD, CHUNK, CHUNK), lambda i, j: (layer, j, 0, 0)),
            pl.BlockSpec((None, tn // HD, CHUNK, HD), lambda i, j: (layer, j, 0, 0)),
        ],
        out_specs=pl.BlockSpec((TM, tn), lambda i, j: (i, j)),
        out_shape=jax.ShapeDtypeStruct((t, D_A), BF16),
        scratch_shapes=[pltpu.VMEM((TM, D_MODEL), BF16)],
        compiler_params=_params("parallel", "arbitrary"),
        name="sgu_branch",
    )(xb, w_in, w_in, ln_g, ln_b, ws, bs_wide)


def _sconv_kernel(x_ref, wb_ref, wc_ref, wh_ref, cw_ref, o_ref, xs_ref, p_ref, carry_ref,
                  *, blocks_per_seq):
    i, j = pl.program_id(0), pl.program_id(1)
    tm = o_ref.shape[0]
    _stage_block(x_ref, xs_ref, j == 0)
    _init_halo(p_ref.at[0:HALO_SHORT], carry_ref.at[j], i % blocks_per_seq == 0)

    for r in range(tm // ROWS):
        x = xs_ref[r * ROWS:(r + 1) * ROWS, :]
        p_ref[HALO_SHORT + r * ROWS:HALO_SHORT + (r + 1) * ROWS, :] = (
            _dot(x, wc_ref[...]) * _dot(x, wh_ref[...]))
        y = _conv3(p_ref, cw_ref[...], HALO_SHORT, r * ROWS, ROWS)
        o_ref[r * ROWS:(r + 1) * ROWS, :] = (_dot(x, wb_ref[...]) * y).astype(o_ref.dtype)
    carry_ref[j] = p_ref[tm:tm + HALO_SHORT, :]


def _sconv_branch(xb, w_in, b_conv, layer, blocks_per_seq):
    t = xb.shape[0]
    tn = TN_FRONT
    nj = D_B // tn
    wspec = lambda off: pl.BlockSpec((None, D_MODEL, tn), lambda i, j: (layer, 0, off // tn + j))
    return pl.pallas_call(
        functools.partial(_sconv_kernel, blocks_per_seq=blocks_per_seq),
        grid=(t // TM, nj),
        in_specs=[
            pl.BlockSpec((TM, D_MODEL), lambda i, j: (i, 0)),
            wspec(OFF_B), wspec(OFF_B + D_B), wspec(OFF_B + 2 * D_B),
            pl.BlockSpec((None, SC_K, tn), lambda i, j: (layer, 0, j)),
        ],
        out_specs=pl.BlockSpec((TM, tn), lambda i, j: (i, j)),
        out_shape=jax.ShapeDtypeStruct((t, D_B), BF16),
        scratch_shapes=[pltpu.VMEM((TM, D_MODEL), BF16),
                        pltpu.VMEM((HALO_SHORT + TM, tn), F32),
                        pltpu.VMEM((nj, HALO_SHORT, tn), F32)],
        compiler_params=_params("arbitrary", "arbitrary"),
        name="sconv_branch",
    )(xb, w_in, w_in, w_in, b_conv)


def _cconv_kernel(x_ref, wa_ref, wg_ref, cw_ref, cb_ref, o_ref, xs_ref, ca_ref, carry_ref,
                  *, blocks_per_seq):
    i, j = pl.program_id(0), pl.program_id(1)
    nslab, tm, _ = o_ref.shape
    _stage_block(x_ref, xs_ref, j == 0)
    _init_halo(ca_ref.at[:, 0:HALO_LONG], carry_ref.at[j], i % blocks_per_seq == 0)

    first = HALO_LONG - (CF_K - 1)

    def conv_group(s, q):
        ls = slice(s * V7X_LANES, (s + 1) * V7X_LANES)
        accs = [jnp.broadcast_to(cb_ref[:, ls], (V7X_SUBLANES, V7X_LANES))] * ROW_STRIDE
        for k in range(CF_K):
            wk = jnp.broadcast_to(cw_ref[k:k + 1, ls], (V7X_SUBLANES, V7X_LANES))
            for r in range(ROW_STRIDE):
                rows = pl.ds(q * ROW_GROUP + r + first + k, V7X_SUBLANES, stride=ROW_STRIDE)
                accs[r] = accs[r] + wk * ca_ref[s, rows, :]
        for r in range(ROW_STRIDE):
            o_ref[s, pl.ds(q * ROW_GROUP + r, V7X_SUBLANES, stride=ROW_STRIDE), :] = accs[r]

    groups_per_part = ROWS // ROW_GROUP

    for part in range(tm // ROWS):
        x = xs_ref[part * ROWS:(part + 1) * ROWS, :]
        ca = _dot(x, wa_ref[...]) * _sigmoid(_dot(x, wg_ref[...]))
        data = slice(HALO_LONG + part * ROWS, HALO_LONG + (part + 1) * ROWS)
        for s in range(nslab):
            ca_ref[s, data, :] = ca[:, s * V7X_LANES:(s + 1) * V7X_LANES]
        for s in range(nslab):
            for q in range(part * groups_per_part, (part + 1) * groups_per_part):
                conv_group(s, q)
    carry_ref[j] = ca_ref[:, tm:tm + HALO_LONG, :]


def _cconv_branch(xb, w_in, c_conv, c_conv_b, layer, blocks_per_seq):
    t = xb.shape[0]
    tn = TN_FRONT
    nj = D_C // tn
    nslab = tn // V7X_LANES
    wspec = lambda off: pl.BlockSpec((None, D_MODEL, tn), lambda i, j: (layer, 0, off // tn + j))
    return pl.pallas_call(
        functools.partial(_cconv_kernel, blocks_per_seq=blocks_per_seq),
        grid=(t // TM, nj),
        in_specs=[
            pl.BlockSpec((TM, D_MODEL), lambda i, j: (i, 0)),
            wspec(OFF_C), wspec(OFF_C + D_C),
            pl.BlockSpec((None, CF_K, tn), lambda i, j: (layer, 0, j)),
            pl.BlockSpec((None, 1, tn), lambda i, j: (layer, 0, j)),
        ],
        out_specs=pl.BlockSpec((nslab, TM, V7X_LANES), lambda i, j: (j, i, 0)),
        out_shape=jax.ShapeDtypeStruct((D_C // V7X_LANES, t, V7X_LANES), F32),
        scratch_shapes=[pltpu.VMEM((TM, D_MODEL), BF16),
                        pltpu.VMEM((nslab, HALO_LONG + TM, V7X_LANES), F32),
                        pltpu.VMEM((nj, nslab, HALO_LONG, V7X_LANES), F32)],
        compiler_params=_params("arbitrary", "arbitrary"),
        name="cconv_branch",
    )(xb, w_in, w_in, c_conv, c_conv_b)


def _ln_silu_kernel(v_ref, g_ref, b_ref, o_ref):
    nslab, _, lanes = v_ref.shape
    width = nslab * lanes
    v = v_ref[...]
    mu = jnp.sum(jnp.sum(v, axis=0), axis=-1, keepdims=True) / width
    dv = v - mu[None]
    var = jnp.sum(jnp.sum(dv * dv, axis=0), axis=-1, keepdims=True) / width
    y = dv * jax.lax.rsqrt(var + LN_EPS)[None] * g_ref[...] + b_ref[...]
    y = y * _sigmoid(y)
    for s in range(nslab):
        o_ref[:, s * lanes:(s + 1) * lanes] = y[s].astype(o_ref.dtype)


def _ln_silu(v, g, b, layer):
    nslab, t, lanes = v.shape
    par_spec = pl.BlockSpec((None, nslab, 1, lanes), lambda i: (layer, 0, 0, 0))
    return pl.pallas_call(
        _ln_silu_kernel,
        grid=(t // TR_LN,),
        in_specs=[pl.BlockSpec((nslab, TR_LN, lanes), lambda i: (0, i, 0)), par_spec, par_spec],
        out_specs=pl.BlockSpec((TR_LN, nslab * lanes), lambda i: (i, 0)),
        out_shape=jax.ShapeDtypeStruct((t, nslab * lanes), BF16),
        compiler_params=_params("parallel"),
        name="ln_silu",
    )(v, g, b)


def _residual_ln_kernel(x_ref, y_ref, g_ref, b_ref, o_ref, ob_ref, *, alpha):
    v = alpha * x_ref[...] + y_ref[...]
    mu = jnp.mean(v, axis=-1, keepdims=True)
    dv = v - mu
    var = jnp.mean(dv * dv, axis=-1, keepdims=True)
    y = dv * jax.lax.rsqrt(var + LN_EPS) * g_ref[...] + b_ref[...]
    o_ref[...] = y
    ob_ref[...] = y.astype(ob_ref.dtype)


def _residual_ln(x, y, g, b, layer, alpha):
    t, c = x.shape
    row_spec = pl.BlockSpec((TR_LN, c), lambda i: (i, 0))
    par_spec = pl.BlockSpec((None, 1, c), lambda i: (layer, 0, 0))
    return pl.pallas_call(
        functools.partial(_residual_ln_kernel, alpha=alpha),
        grid=(t // TR_LN,),
        in_specs=[row_spec, row_spec, par_spec, par_spec],
        out_specs=[row_spec, row_spec],
        out_shape=[jax.ShapeDtypeStruct((t, c), F32), jax.ShapeDtypeStruct((t, c), BF16)],
        compiler_params=_params("parallel"),
        name="residual_ln",
    )(x, y, g, b)


def _merge_kernel(x_ref, a_ref, b_ref, c_ref, wg0_ref, wg1_ref, wg2_ref, wa_ref, wb_ref, wc_ref,
                  bg_ref, o_ref):
    x = x_ref[...]
    g0 = _sigmoid(_dot(x, wg0_ref[...]) + bg_ref[0:1, :])
    m = g0 * _dot(a_ref[...], wa_ref[...])
    g1 = _sigmoid(_dot(x, wg1_ref[...]) + bg_ref[1:2, :])
    m = m + g1 * _dot(b_ref[...], wb_ref[...])
    g2 = _sigmoid(_dot(x, wg2_ref[...]) + bg_ref[2:3, :])
    m = m + g2 * _dot(c_ref[...], wc_ref[...])
    o_ref[...] = m.astype(o_ref.dtype)


def _merge(xb, act_a, act_b, act_c, w_in, a_out, b_out, c_out, b_gate, layer):
    t = xb.shape[0]
    tn = TN_MERGE
    resident = dict(pipeline_mode=pl.Buffered(1))
    gspec = lambda k: pl.BlockSpec((None, D_MODEL, tn),
                                   lambda i, j: (layer, 0, (OFF_G + k * D_MODEL) // tn + j))
    ospec = lambda width: pl.BlockSpec((None, width, tn), lambda i, j: (layer, 0, j))
    return pl.pallas_call(
        _merge_kernel,
        grid=(t // TM, D_MODEL // tn),
        in_specs=[
            pl.BlockSpec((TM, D_MODEL), lambda i, j: (i, 0), **resident),
            pl.BlockSpec((TM, D_A), lambda i, j: (i, 0), **resident),
            pl.BlockSpec((TM, D_B), lambda i, j: (i, 0), **resident),
            pl.BlockSpec((TM, D_C), lambda i, j: (i, 0), **resident),
            gspec(0), gspec(1), gspec(2),
            ospec(D_A), ospec(D_B), ospec(D_C),
            pl.BlockSpec((None, N_BRANCH, tn), lambda i, j: (layer, 0, j)),
        ],
        out_specs=pl.BlockSpec((TM, tn), lambda i, j: (i, j)),
        out_shape=jax.ShapeDtypeStruct((t, D_MODEL), BF16),
        compiler_params=_params("parallel", "arbitrary"),
        name="gate_merge",
    )(xb, act_a, act_b, act_c, w_in, w_in, w_in, a_out, b_out, c_out, b_gate)


def _proj_kernel(m_ref, w_ref, o_ref):
    o_ref[...] = _dot(m_ref[...], w_ref[...])


def _out_proj(m, w_o, layer):
    t = m.shape[0]
    tn = TN_OUT
    return pl.pallas_call(
        _proj_kernel,
        grid=(t // TM, D_MODEL // tn),
        in_specs=[pl.BlockSpec((TM, D_MODEL), lambda i, j: (i, 0)),
                  pl.BlockSpec((None, D_MODEL, tn), lambda i, j: (layer, 0, j))],
        out_specs=pl.BlockSpec((TM, tn), lambda i, j: (i, j)),
        out_shape=jax.ShapeDtypeStruct((t, D_MODEL), F32),
        compiler_params=_params("parallel", "arbitrary"),
        name="out_proj",
    )(m, w_o)


def _up_tile(n, nj, ntiles):
    nu = jnp.minimum(n, ntiles - 1)
    return nu // nj, nu % nj


def _down_tile(n, nj):
    nd = jnp.maximum(n - 1, 0)
    return nd // nj, nd % nj


def _ffn_kernel(x_ref, w1_ref, w2_ref, cw1_ref, cw2_ref, wd_ref, o_ref,
                xs_ref, h1_ref, h2_ref, act0_ref, act1_ref, carry_ref, *, nj, ntiles, blocks_per_seq):
    n = pl.program_id(0)
    iu, ju = _up_tile(n, nj, ntiles)
    _, jd = _down_tile(n, nj)
    tm = o_ref.shape[0]

    @pl.when(n == 0)
    def _():
        act1_ref[...] = jnp.zeros(act1_ref.shape, act1_ref.dtype)

    @pl.when(jd == 0)
    def _():
        o_ref[...] = jnp.zeros(o_ref.shape, F32)

    _stage_block(x_ref, xs_ref, ju == 0)
    seq_start = iu % blocks_per_seq == 0
    _init_halo(h1_ref.at[0:HALO_SHORT], carry_ref.at[ju, 0], seq_start)
    _init_halo(h2_ref.at[0:HALO_SHORT], carry_ref.at[ju, 1], seq_start)

    def body(act_in, act_out):
        for r in range(tm // ROWS):
            x = xs_ref[r * ROWS:(r + 1) * ROWS, :]
            data = slice(HALO_SHORT + r * ROWS, HALO_SHORT + (r + 1) * ROWS)
            h1_ref[data, :] = _dot(x, w1_ref[...])
            h2_ref[data, :] = _dot(x, w2_ref[...])
            c1 = _conv3(h1_ref, cw1_ref[...], HALO_SHORT, r * ROWS, ROWS)
            c2 = _conv3(h2_ref, cw2_ref[...], HALO_SHORT, r * ROWS, ROWS)
            act_out[r * ROWS:(r + 1) * ROWS, :] = (c1 * _sigmoid(c1) * c2).astype(act_out.dtype)
        carry_ref[ju, 0] = h1_ref[tm:tm + HALO_SHORT, :]
        carry_ref[ju, 1] = h2_ref[tm:tm + HALO_SHORT, :]

        act = act_in[...]
        for c in range(D_MODEL // DOWN_COLS):
            cs = slice(c * DOWN_COLS, (c + 1) * DOWN_COLS)
            o_ref[:, cs] += _dot(act, wd_ref[:, cs])

    @pl.when(n % 2 == 0)
    def _():
        body(act1_ref, act0_ref)

    @pl.when(n % 2 == 1)
    def _():
        body(act0_ref, act1_ref)


def _conv_ffn(xb, f_up, f_conv, f_down, layer, blocks_per_seq):
    t = xb.shape[0]
    nj = D_FF // TF
    ntiles = (t // TM) * nj
    up = lambda n: _up_tile(n, nj, ntiles)
    down = lambda n: _down_tile(n, nj)
    return pl.pallas_call(
        functools.partial(_ffn_kernel, nj=nj, ntiles=ntiles, blocks_per_seq=blocks_per_seq),
        grid=(ntiles + 1,),
        in_specs=[
            pl.BlockSpec((TM, D_MODEL), lambda n: (up(n)[0], 0), pipeline_mode=pl.Buffered(1)),
            pl.BlockSpec((None, D_MODEL, TF), lambda n: (layer, 0, up(n)[1])),
            pl.BlockSpec((None, D_MODEL, TF), lambda n: (layer, 0, nj + up(n)[1])),
            pl.BlockSpec((None, FF_K, TF), lambda n: (layer, 0, up(n)[1])),
            pl.BlockSpec((None, FF_K, TF), lambda n: (layer, 0, nj + up(n)[1])),
            pl.BlockSpec((None, TF, D_MODEL), lambda n: (layer, down(n)[1], 0)),
        ],
        out_specs=pl.BlockSpec((TM, D_MODEL), lambda n: (down(n)[0], 0),
                               pipeline_mode=pl.Buffered(1)),
        out_shape=jax.ShapeDtypeStruct((t, D_MODEL), F32),
        scratch_shapes=[pltpu.VMEM((TM, D_MODEL), BF16)]
                       + [pltpu.VMEM((HALO_SHORT + TM, TF), F32)] * 2
                       + [pltpu.VMEM((TM, TF), BF16)] * 2
                       + [pltpu.VMEM((nj, 2, HALO_SHORT, TF), F32)],
        compiler_params=_params("arbitrary"),
        name="conv_ffn",
    )(xb, f_up, f_up, f_conv, f_conv, f_down)


def kernel(x, w_in, b_gate, a_ln_g, a_ln_b, a_ws, a_bs, a_out, b_conv, b_out, c_conv, c_conv_b,
           c_ln_g, c_ln_b, c_out, w_o, ln1_g, ln1_b, f_up, f_conv, f_down, ln2_g, ln2_b):
    bsz, seq, d = x.shape
    depth = w_in.shape[0]
    assert d == D_MODEL and seq % TM == 0 and w_in.shape[-1] == N_IN and f_down.shape[1] == D_FF
    alpha = (2.0 * depth) ** 0.25
    blocks_per_seq = seq // TM
    t = bsz * seq

    as_bf16 = lambda w: w.astype(BF16)
    w_in_b, a_out_b, b_out_b, c_out_b, w_o_b, f_up_b, f_down_b = map(
        as_bf16, (w_in, a_out, b_out, c_out, w_o, f_up, f_down))
    row3 = lambda p: p[:, None, :]
    a_ln_g3, a_ln_b3, c_conv_b3 = map(row3, (a_ln_g, a_ln_b, c_conv_b))
    ln1_g3, ln1_b3, ln2_g3, ln2_b3 = map(row3, (ln1_g, ln1_b, ln2_g, ln2_b))
    slabs = lambda p: p.reshape(depth, D_C // V7X_LANES, 1, V7X_LANES)
    c_ln_g4, c_ln_b4 = slabs(c_ln_g), slabs(c_ln_b)
    bs_wide = jnp.broadcast_to(a_bs[..., None], a_bs.shape + (HD,))

    xf = x.reshape(t, d)
    xb = xf.astype(BF16)
    for l in range(depth):
        act_a = _sgu_branch(xb, w_in_b, a_ln_g3, a_ln_b3, a_ws, bs_wide, l)
        act_b = _sconv_branch(xb, w_in_b, b_conv, l, blocks_per_seq)
        pre_c = _cconv_branch(xb, w_in_b, c_conv, c_conv_b3, l, blocks_per_seq)
        act_c = _ln_silu(pre_c, c_ln_g4, c_ln_b4, l)
        m = _merge(xb, act_a, act_b, act_c, w_in_b, a_out_b, b_out_b, c_out_b, b_gate, l)
        mix = _out_proj(m, w_o_b, l)
        xf, xb = _residual_ln(xf, mix, ln1_g3, ln1_b3, l, alpha)
        ffn = _conv_ffn(xb, f_up_b, f_conv, f_down_b, l, blocks_per_seq)
        xf, xb = _residual_ln(xf, ffn, ln2_g3, ln2_b3, l, alpha)
    return xf.reshape(bsz, seq, d)
```

```python
import functools
import math

import jax
import jax.numpy as jnp
from jax.experimental import pallas as pl
from jax.experimental.pallas import tpu as pltpu

F32 = jnp.float32
BF16 = jnp.bfloat16

D_MODEL = 4096
CHUNK = 128
HD = 128
D_A = D_MODEL // 2
H_A = D_A // HD
D_B = D_MODEL // 2
SC_K = 3
D_C = D_MODEL // 2
CF_K = 31
D_FF = 11008
FF_K = 3
N_BRANCH = 3
LN_EPS = 1e-5
OFF_A = 0
OFF_B = OFF_A + 2 * D_A
OFF_C = OFF_B + 3 * D_B
OFF_G = OFF_C + 2 * D_C
N_IN = OFF_G + N_BRANCH * D_MODEL

V7X_SUBLANES = 8
V7X_LANES = 128
V7X_VMEM_BYTES = 64 << 20

TM = 1024
ROWS = 128
TN_FRONT = 256
TN_MERGE = 256
TN_OUT = 512
TF = 256
TN_DOWN = 256
TR_LN = 256
HALO_SHORT = V7X_SUBLANES
HALO_LONG = 32
ROW_STRIDE = 4
ROW_GROUP = ROW_STRIDE * V7X_SUBLANES
VMEM_LIMIT = V7X_VMEM_BYTES - (8 << 20)

assert ROWS == CHUNK and TM % ROWS == 0 and ROWS % ROW_GROUP == 0


def _sigmoid(v):
    return 1.0 / (1.0 + jnp.exp(-v))


def _gelu_tanh(v):
    c = math.sqrt(2.0 / math.pi)
    return 0.5 * v * (1.0 + jnp.tanh(c * (v + 0.044715 * (v * v * v))))


def _dot(a, b):
    return jnp.dot(a, b, preferred_element_type=F32)


def _params(*sem):
    return pltpu.CompilerParams(dimension_semantics=sem, vmem_limit_bytes=VMEM_LIMIT)


def _tile_major(w, tn):
    l, k, n = w.shape
    return w.reshape(l, k, n // tn, tn).transpose(0, 2, 1, 3).astype(BF16)


def _col_tile(k, tn, index_map):
    return pl.BlockSpec((None, None, k, tn), lambda *g: (*index_map(*g), 0, 0))


def _stage_block(x_ref, xs_ref, first_use):
    @pl.when(first_use)
    def _():
        xs_ref[...] = x_ref[...]


def _init_halo(stage_halo, carry, seq_start):
    @pl.when(seq_start)
    def _():
        stage_halo[...] = jnp.zeros(stage_halo.shape, F32)

    @pl.when(jnp.logical_not(seq_start))
    def _():
        stage_halo[...] = carry[...]


def _conv3(stage, cw, halo, start, rows):
    at = lambda shift: stage[halo + start - shift:halo + start - shift + rows, :]
    return cw[0:1, :] * at(2) + cw[1:2, :] * at(1) + cw[2:3, :] * at(0)


def _sgu_kernel(x_ref, wu_ref, wv_ref, g_ref, b_ref, ws_ref, bs_ref, o_ref, xs_ref):
    tm, tn = o_ref.shape
    _stage_block(x_ref, xs_ref, pl.program_id(1) == 0)

    row = jax.lax.broadcasted_iota(jnp.int32, (CHUNK, CHUNK), 0)
    col = jax.lax.broadcasted_iota(jnp.int32, (CHUNK, CHUNK), 1)
    causal = row >= col
    heads = range(tn // HD)
    mix_w = [jnp.where(causal, ws_ref[h], 0.0).astype(BF16) for h in heads]

    def project(c):
        x = xs_ref[c * CHUNK:(c + 1) * CHUNK, :]
        return _dot(x, wu_ref[...]), _dot(x, wv_ref[...])

    def normalise(zv):
        v = _gelu_tanh(zv)
        out = []
        for h in heads:
            hs = slice(h * HD, (h + 1) * HD)
            vh = v[:, hs]
            mu = jnp.mean(vh, axis=-1, keepdims=True)
            dv = vh - mu
            var = jnp.mean(dv * dv, axis=-1, keepdims=True)
            out.append((dv * jax.lax.rsqrt(var + LN_EPS) * g_ref[:, hs] + b_ref[:, hs]).astype(BF16))
        return out

    def mix(c, zu, vn):
        u = _gelu_tanh(zu)
        for h in heads:
            hs = slice(h * HD, (h + 1) * HD)
            mixed = _dot(mix_w[h], vn[h]) + bs_ref[h]
            o_ref[c * CHUNK:(c + 1) * CHUNK, hs] = (u[:, hs] * mixed).astype(o_ref.dtype)

    zu, zv = project(0)
    vn = normalise(zv)
    for c in range(1, tm // CHUNK):
        zu_next, zv_next = project(c)
        mix(c - 1, zu, vn)
        vn = normalise(zv_next)
        zu = zu_next
    mix(tm // CHUNK - 1, zu, vn)


def _sgu_branch(xb, w_in, ln_g, ln_b, ws, bs_wide, layer):
    t = xb.shape[0]
    tn = TN_FRONT
    return pl.pallas_call(
        _sgu_kernel,
        grid=(t // TM, D_A // tn),
        in_specs=[
            pl.BlockSpec((TM, D_MODEL), lambda i, j: (i, 0)),
            _col_tile(D_MODEL, tn, lambda i, j: (layer, OFF_A // tn + j)),
            _col_tile(D_MODEL, tn, lambda i, j: (layer, (OFF_A + D_A) // tn + j)),
            pl.BlockSpec((None, 1, tn), lambda i, j: (layer, 0, j)),
            pl.BlockSpec((None, 1, tn), lambda i, j: (layer, 0, j)),
            pl.BlockSpec((None, tn // HD, CHUNK, CHUNK), lambda i, j: (layer, j, 0, 0)),
            pl.BlockSpec((None, tn // HD, CHUNK, HD), lambda i, j: (layer, j, 0, 0)),
        ],
        out_specs=pl.BlockSpec((TM, tn), lambda i, j: (i, j)),
        out_shape=jax.ShapeDtypeStruct((t, D_A), BF16),
        scratch_shapes=[pltpu.VMEM((TM, D_MODEL), BF16)],
        compiler_params=_params("parallel", "arbitrary"),
        name="sgu_branch",
    )(xb, w_in, w_in, ln_g, ln_b, ws, bs_wide)


def _sconv_kernel(x_ref, wb_ref, wc_ref, wh_ref, cw_ref, o_ref, xs_ref, p_ref, carry_ref,
                  *, blocks_per_seq):
    i, j = pl.program_id(0), pl.program_id(1)
    tm = o_ref.shape[0]
    _stage_block(x_ref, xs_ref, j == 0)
    _init_halo(p_ref.at[0:HALO_SHORT], carry_ref.at[j], i % blocks_per_seq == 0)

    for r in range(tm // ROWS):
        x = xs_ref[r * ROWS:(r + 1) * ROWS, :]
        p_ref[HALO_SHORT + r * ROWS:HALO_SHORT + (r + 1) * ROWS, :] = (
            _dot(x, wc_ref[...]) * _dot(x, wh_ref[...]))
        y = _conv3(p_ref, cw_ref[...], HALO_SHORT, r * ROWS, ROWS)
        o_ref[r * ROWS:(r + 1) * ROWS, :] = (_dot(x, wb_ref[...]) * y).astype(o_ref.dtype)
    carry_ref[j] = p_ref[tm:tm + HALO_SHORT, :]


def _sconv_branch(xb, w_in, b_conv, layer, blocks_per_seq):
    t = xb.shape[0]
    tn = TN_FRONT
    nj = D_B // tn
    wspec = lambda off: _col_tile(D_MODEL, tn, lambda i, j: (layer, off // tn + j))
    return pl.pallas_call(
        functools.partial(_sconv_kernel, blocks_per_seq=blocks_per_seq),
        grid=(t // TM, nj),
        in_specs=[
            pl.BlockSpec((TM, D_MODEL), lambda i, j: (i, 0)),
            wspec(OFF_B), wspec(OFF_B + D_B), wspec(OFF_B + 2 * D_B),
            pl.BlockSpec((None, SC_K, tn), lambda i, j: (layer, 0, j)),
        ],
        out_specs=pl.BlockSpec((TM, tn), lambda i, j: (i, j)),
        out_shape=jax.ShapeDtypeStruct((t, D_B), BF16),
        scratch_shapes=[pltpu.VMEM((TM, D_MODEL), BF16),
                        pltpu.VMEM((HALO_SHORT + TM, tn), F32),
                        pltpu.VMEM((nj, HALO_SHORT, tn), F32)],
        compiler_params=_params("arbitrary", "arbitrary"),
        name="sconv_branch",
    )(xb, w_in, w_in, w_in, b_conv)


def _cconv_kernel(x_ref, wa_ref, wg_ref, cw_ref, cb_ref, o_ref, xs_ref, ca_ref, carry_ref,
                  *, blocks_per_seq):
    i, j = pl.program_id(0), pl.program_id(1)
    nslab, tm, _ = o_ref.shape
    _stage_block(x_ref, xs_ref, j == 0)
    _init_halo(ca_ref.at[:, 0:HALO_LONG], carry_ref.at[j], i % blocks_per_seq == 0)

    first = HALO_LONG - (CF_K - 1)

    def conv_group(s, q):
        ls = slice(s * V7X_LANES, (s + 1) * V7X_LANES)
        accs = [jnp.broadcast_to(cb_ref[:, ls], (V7X_SUBLANES, V7X_LANES))] * ROW_STRIDE
        for k in range(CF_K):
            wk = jnp.broadcast_to(cw_ref[k:k + 1, ls], (V7X_SUBLANES, V7X_LANES))
            for r in range(ROW_STRIDE):
                rows = pl.ds(q * ROW_GROUP + r + first + k, V7X_SUBLANES, stride=ROW_STRIDE)
                accs[r] = accs[r] + wk * ca_ref[s, rows, :]
        for r in range(ROW_STRIDE):
            o_ref[s, pl.ds(q * ROW_GROUP + r, V7X_SUBLANES, stride=ROW_STRIDE), :] = accs[r]

    groups_per_part = ROWS // ROW_GROUP

    for part in range(tm // ROWS):
        x = xs_ref[part * ROWS:(part + 1) * ROWS, :]
        ca = _dot(x, wa_ref[...]) * _sigmoid(_dot(x, wg_ref[...]))
        data = slice(HALO_LONG + part * ROWS, HALO_LONG + (part + 1) * ROWS)
        for s in range(nslab):
            ca_ref[s, data, :] = ca[:, s * V7X_LANES:(s + 1) * V7X_LANES]
        for s in range(nslab):
            for q in range(part * groups_per_part, (part + 1) * groups_per_part):
                conv_group(s, q)
    carry_ref[j] = ca_ref[:, tm:tm + HALO_LONG, :]


def _cconv_branch(xb, w_in, c_conv, c_conv_b, layer, blocks_per_seq):
    t = xb.shape[0]
    tn = TN_FRONT
    nj = D_C // tn
    nslab = tn // V7X_LANES
    wspec = lambda off: _col_tile(D_MODEL, tn, lambda i, j: (layer, off // tn + j))
    return pl.pallas_call(
        functools.partial(_cconv_kernel, blocks_per_seq=blocks_per_seq),
        grid=(t // TM, nj),
        in_specs=[
            pl.BlockSpec((TM, D_MODEL), lambda i, j: (i, 0)),
            wspec(OFF_C), wspec(OFF_C + D_C),
            pl.BlockSpec((None, CF_K, tn), lambda i, j: (layer, 0, j)),
            pl.BlockSpec((None, 1, tn), lambda i, j: (layer, 0, j)),
        ],
        out_specs=pl.BlockSpec((nslab, TM, V7X_LANES), lambda i, j: (j, i, 0)),
        out_shape=jax.ShapeDtypeStruct((D_C // V7X_LANES, t, V7X_LANES), F32),
        scratch_shapes=[pltpu.VMEM((TM, D_MODEL), BF16),
                        pltpu.VMEM((nslab, HALO_LONG + TM, V7X_LANES), F32),
                        pltpu.VMEM((nj, nslab, HALO_LONG, V7X_LANES), F32)],
        compiler_params=_params("arbitrary", "arbitrary"),
        name="cconv_branch",
    )(xb, w_in, w_in, c_conv, c_conv_b)


def _ln_silu_kernel(v_ref, g_ref, b_ref, o_ref):
    nslab, _, lanes = v_ref.shape
    width = nslab * lanes
    v = v_ref[...]
    mu = jnp.sum(jnp.sum(v, axis=0), axis=-1, keepdims=True) / width
    dv = v - mu[None]
    var = jnp.sum(jnp.sum(dv * dv, axis=0), axis=-1, keepdims=True) / width
    y = dv * jax.lax.rsqrt(var + LN_EPS)[None] * g_ref[...] + b_ref[...]
    y = y * _sigmoid(y)
    for s in range(nslab):
        o_ref[:, s * lanes:(s + 1) * lanes] = y[s].astype(o_ref.dtype)


def _ln_silu(v, g, b, layer):
    nslab, t, lanes = v.shape
    par_spec = pl.BlockSpec((None, nslab, 1, lanes), lambda i: (layer, 0, 0, 0))
    return pl.pallas_call(
        _ln_silu_kernel,
        grid=(t // TR_LN,),
        in_specs=[pl.BlockSpec((nslab, TR_LN, lanes), lambda i: (0, i, 0)), par_spec, par_spec],
        out_specs=pl.BlockSpec((TR_LN, nslab * lanes), lambda i: (i, 0)),
        out_shape=jax.ShapeDtypeStruct((t, nslab * lanes), BF16),
        compiler_params=_params("parallel"),
        name="ln_silu",
    )(v, g, b)


def _ln_rows_kernel(v_ref, g_ref, b_ref, o_ref, ob_ref):
    v = v_ref[...]
    mu = jnp.mean(v, axis=-1, keepdims=True)
    dv = v - mu
    var = jnp.mean(dv * dv, axis=-1, keepdims=True)
    y = dv * jax.lax.rsqrt(var + LN_EPS) * g_ref[...] + b_ref[...]
    o_ref[...] = y
    ob_ref[...] = y.astype(ob_ref.dtype)


def _ln_rows(v, g, b, layer):
    t, c = v.shape
    row_spec = pl.BlockSpec((TR_LN, c), lambda i: (i, 0))
    par_spec = pl.BlockSpec((None, 1, c), lambda i: (layer, 0, 0))
    return pl.pallas_call(
        _ln_rows_kernel,
        grid=(t // TR_LN,),
        in_specs=[row_spec, par_spec, par_spec],
        out_specs=[row_spec, row_spec],
        out_shape=[jax.ShapeDtypeStruct((t, c), F32), jax.ShapeDtypeStruct((t, c), BF16)],
        compiler_params=_params("parallel"),
        name="ln_rows",
    )(v, g, b)


def _merge_kernel(x_ref, a_ref, b_ref, c_ref, wg0_ref, wg1_ref, wg2_ref, wa_ref, wb_ref, wc_ref,
                  bg_ref, o_ref):
    x = x_ref[...]
    g0 = _sigmoid(_dot(x, wg0_ref[...]) + bg_ref[0:1, :])
    m = g0 * _dot(a_ref[...], wa_ref[...])
    g1 = _sigmoid(_dot(x, wg1_ref[...]) + bg_ref[1:2, :])
    m = m + g1 * _dot(b_ref[...], wb_ref[...])
    g2 = _sigmoid(_dot(x, wg2_ref[...]) + bg_ref[2:3, :])
    m = m + g2 * _dot(c_ref[...], wc_ref[...])
    o_ref[...] = m.astype(o_ref.dtype)


def _merge(xb, act_a, act_b, act_c, w_in, a_out, b_out, c_out, b_gate, layer):
    t = xb.shape[0]
    tn = TN_MERGE
    resident = dict(pipeline_mode=pl.Buffered(1))
    gspec = lambda k: _col_tile(D_MODEL, tn, lambda i, j: (layer, (OFF_G + k * D_MODEL) // tn + j))
    ospec = lambda width: _col_tile(width, tn, lambda i, j: (layer, j))
    return pl.pallas_call(
        _merge_kernel,
        grid=(t // TM, D_MODEL // tn),
        in_specs=[
            pl.BlockSpec((TM, D_MODEL), lambda i, j: (i, 0), **resident),
            pl.BlockSpec((TM, D_A), lambda i, j: (i, 0), **resident),
            pl.BlockSpec((TM, D_B), lambda i, j: (i, 0), **resident),
            pl.BlockSpec((TM, D_C), lambda i, j: (i, 0), **resident),
            gspec(0), gspec(1), gspec(2),
            ospec(D_A), ospec(D_B), ospec(D_C),
            pl.BlockSpec((None, N_BRANCH, tn), lambda i, j: (layer, 0, j)),
        ],
        out_specs=pl.BlockSpec((TM, tn), lambda i, j: (i, j)),
        out_shape=jax.ShapeDtypeStruct((t, D_MODEL), BF16),
        compiler_params=_params("parallel", "arbitrary"),
        name="gate_merge",
    )(xb, act_a, act_b, act_c, w_in, w_in, w_in, a_out, b_out, c_out, b_gate)


def _proj_residual_kernel(a_ref, w_ref, x_ref, o_ref, *, alpha):
    half = a_ref.shape[0] // 2
    for rows in (slice(0, half), slice(half, 2 * half)):
        o_ref[rows, :] = alpha * x_ref[rows, :] + _dot(a_ref[rows, :], w_ref[...])


def _proj_residual(a, w, x, layer, alpha, tn, name):
    t, k = a.shape
    return pl.pallas_call(
        functools.partial(_proj_residual_kernel, alpha=alpha),
        grid=(t // TM, D_MODEL // tn),
        in_specs=[pl.BlockSpec((TM, k), lambda i, j: (i, 0), pipeline_mode=pl.Buffered(1)),
                  _col_tile(k, tn, lambda i, j: (layer, j)),
                  pl.BlockSpec((TM, tn), lambda i, j: (i, j))],
        out_specs=pl.BlockSpec((TM, tn), lambda i, j: (i, j)),
        out_shape=jax.ShapeDtypeStruct((t, D_MODEL), F32),
        compiler_params=_params("parallel", "arbitrary"),
        name=name,
    )(a, w, x)


def _ffn_up_kernel(x_ref, w1_ref, w2_ref, cw1_ref, cw2_ref, o_ref, xs_ref, h1_ref, h2_ref,
                   carry_ref, *, blocks_per_seq):
    i, j = pl.program_id(0), pl.program_id(1)
    tm = o_ref.shape[0]
    _stage_block(x_ref, xs_ref, j == 0)
    seq_start = i % blocks_per_seq == 0
    _init_halo(h1_ref.at[0:HALO_SHORT], carry_ref.at[j, 0], seq_start)
    _init_halo(h2_ref.at[0:HALO_SHORT], carry_ref.at[j, 1], seq_start)

    for r in range(tm // ROWS):
        x = xs_ref[r * ROWS:(r + 1) * ROWS, :]
        data = slice(HALO_SHORT + r * ROWS, HALO_SHORT + (r + 1) * ROWS)
        h1_ref[data, :] = _dot(x, w1_ref[...])
        h2_ref[data, :] = _dot(x, w2_ref[...])
        c1 = _conv3(h1_ref, cw1_ref[...], HALO_SHORT, r * ROWS, ROWS)
        c2 = _conv3(h2_ref, cw2_ref[...], HALO_SHORT, r * ROWS, ROWS)
        o_ref[r * ROWS:(r + 1) * ROWS, :] = (c1 * _sigmoid(c1) * c2).astype(o_ref.dtype)
    carry_ref[j, 0] = h1_ref[tm:tm + HALO_SHORT, :]
    carry_ref[j, 1] = h2_ref[tm:tm + HALO_SHORT, :]


def _ffn_up(xb, f_up, f_conv, layer, blocks_per_seq):
    t = xb.shape[0]
    nj = D_FF // TF
    return pl.pallas_call(
        functools.partial(_ffn_up_kernel, blocks_per_seq=blocks_per_seq),
        grid=(t // TM, nj),
        in_specs=[
            pl.BlockSpec((TM, D_MODEL), lambda i, j: (i, 0)),
            _col_tile(D_MODEL, TF, lambda i, j: (layer, j)),
            _col_tile(D_MODEL, TF, lambda i, j: (layer, nj + j)),
            pl.BlockSpec((None, FF_K, TF), lambda i, j: (layer, 0, j)),
            pl.BlockSpec((None, FF_K, TF), lambda i, j: (layer, 0, nj + j)),
        ],
        out_specs=pl.BlockSpec((TM, TF), lambda i, j: (i, j)),
        out_shape=jax.ShapeDtypeStruct((t, D_FF), BF16),
        scratch_shapes=[pltpu.VMEM((TM, D_MODEL), BF16),
                        pltpu.VMEM((HALO_SHORT + TM, TF), F32),
                        pltpu.VMEM((HALO_SHORT + TM, TF), F32),
                        pltpu.VMEM((nj, 2, HALO_SHORT, TF), F32)],
        compiler_params=_params("arbitrary", "arbitrary"),
        name="ffn_up",
    )(xb, f_up, f_up, f_conv, f_conv)


def kernel(x, w_in, b_gate, a_ln_g, a_ln_b, a_ws, a_bs, a_out, b_conv, b_out, c_conv, c_conv_b,
           c_ln_g, c_ln_b, c_out, w_o, ln1_g, ln1_b, f_up, f_conv, f_down, ln2_g, ln2_b):
    bsz, seq, d = x.shape
    depth = w_in.shape[0]
    assert d == D_MODEL and seq % TM == 0 and w_in.shape[-1] == N_IN and f_down.shape[1] == D_FF
    alpha = (2.0 * depth) ** 0.25
    blocks_per_seq = seq // TM
    t = bsz * seq

    w_in_t = _tile_major(w_in, TN_FRONT)
    a_out_t, b_out_t, c_out_t = (_tile_major(w, TN_MERGE) for w in (a_out, b_out, c_out))
    w_o_t = _tile_major(w_o, TN_OUT)
    f_up_t = _tile_major(f_up, TF)
    f_down_t = _tile_major(f_down, TN_DOWN)
    row3 = lambda p: p[:, None, :]
    a_ln_g3, a_ln_b3, c_conv_b3 = map(row3, (a_ln_g, a_ln_b, c_conv_b))
    ln1_g3, ln1_b3, ln2_g3, ln2_b3 = map(row3, (ln1_g, ln1_b, ln2_g, ln2_b))
    slabs = lambda p: p.reshape(depth, D_C // V7X_LANES, 1, V7X_LANES)
    c_ln_g4, c_ln_b4 = slabs(c_ln_g), slabs(c_ln_b)
    bs_wide = jnp.broadcast_to(a_bs[..., None], a_bs.shape + (HD,))

    xf = x.reshape(t, d)
    xb = xf.astype(BF16)
    for l in range(depth):
        act_a = _sgu_branch(xb, w_in_t, a_ln_g3, a_ln_b3, a_ws, bs_wide, l)
        act_b = _sconv_branch(xb, w_in_t, b_conv, l, blocks_per_seq)
        pre_c = _cconv_branch(xb, w_in_t, c_conv, c_conv_b3, l, blocks_per_seq)
        act_c = _ln_silu(pre_c, c_ln_g4, c_ln_b4, l)
        m = _merge(xb, act_a, act_b, act_c, w_in_t, a_out_t, b_out_t, c_out_t, b_gate, l)
        mixed = _proj_residual(m, w_o_t, xf, l, alpha, TN_OUT, "out_proj")
        xf, xb = _ln_rows(mixed, ln1_g3, ln1_b3, l)
        act = _ffn_up(xb, f_up_t, f_conv, l, blocks_per_seq)
        ffn = _proj_residual(act, f_down_t, xf, l, alpha, TN_DOWN, "ffn_down")
        xf, xb = _ln_rows(ffn, ln2_g3, ln2_b3, l)
    return xf.reshape(bsz, seq, d)
```

```python
import functools
import math

import jax
import jax.numpy as jnp
from jax.experimental import pallas as pl
from jax.experimental.pallas import tpu as pltpu

F32 = jnp.float32
BF16 = jnp.bfloat16

D_MODEL = 4096
CHUNK = 128
HD = 128
D_A = D_MODEL // 2
H_A = D_A // HD
D_B = D_MODEL // 2
SC_K = 3
D_C = D_MODEL // 2
CF_K = 31
D_FF = 11008
FF_K = 3
N_BRANCH = 3
LN_EPS = 1e-5
OFF_A = 0
OFF_B = OFF_A + 2 * D_A
OFF_C = OFF_B + 3 * D_B
OFF_G = OFF_C + 2 * D_C
N_IN = OFF_G + N_BRANCH * D_MODEL

V7X_SUBLANES = 8
V7X_LANES = 128
BF16_ROWS = 2 * V7X_SUBLANES
V7X_VMEM_BYTES = 64 << 20

TM = 1024
ROWS = 256
TN_FRONT = 256
TN_MERGE = 256
TN_OUT = 512
TF = 256
TN_DOWN = 256
TR_LN = 256
HALO_SHORT = V7X_SUBLANES
HALO_LONG = 32
ROW_STRIDE = 4
ROW_GROUP = ROW_STRIDE * V7X_SUBLANES
VMEM_LIMIT = V7X_VMEM_BYTES - (8 << 20)

assert ROWS % CHUNK == 0 and TM % ROWS == 0 and ROWS % ROW_GROUP == 0


def _sigmoid(v):
    return 0.5 * (1.0 + jnp.tanh(0.5 * v))


def _gelu_tanh(v):
    c = math.sqrt(2.0 / math.pi)
    return 0.5 * v * (1.0 + jnp.tanh(c * (v + 0.044715 * (v * v * v))))


def _dot(a, b):
    return jnp.dot(a, b, preferred_element_type=F32)


def _params(*sem):
    return pltpu.CompilerParams(dimension_semantics=sem, vmem_limit_bytes=VMEM_LIMIT)


def _col_tile(k, tn, tile_of):
    return pl.BlockSpec((k, tn), lambda *g: (0, tile_of(*g)))


def _cast_jobs(weights, layer, steps):
    in_specs, out_specs, out_shapes = [], [], []
    for w in weights:
        _, r, c = w.shape
        rows = max(BF16_ROWS, -(-r // steps // BF16_ROWS) * BF16_ROWS)
        last = r // rows - 1
        assert r % rows == 0
        in_specs.append(lambda step_of, rows=rows, c=c, last=last: pl.BlockSpec(
            (None, rows, c), lambda *g: (layer, jnp.minimum(step_of(*g), last), 0)))
        out_specs.append(lambda step_of, rows=rows, c=c, last=last: pl.BlockSpec(
            (rows, c), lambda *g: (jnp.minimum(step_of(*g), last), 0)))
        out_shapes.append(jax.ShapeDtypeStruct((r, c), BF16))
    return in_specs, out_specs, out_shapes


def _cast_part(src_refs, dst_refs, part, nparts):
    for src, dst in zip(src_refs, dst_refs):
        c = src.shape[-1] // nparts
        dst[:, part * c:(part + 1) * c] = src[:, part * c:(part + 1) * c].astype(dst.dtype)


def _stage_block(x_ref, xs_ref, first_use):
    @pl.when(first_use)
    def _():
        xs_ref[...] = x_ref[...]


def _init_halo(stage_halo, carry, seq_start):
    @pl.when(seq_start)
    def _():
        stage_halo[...] = jnp.zeros(stage_halo.shape, F32)

    @pl.when(jnp.logical_not(seq_start))
    def _():
        stage_halo[...] = carry[...]


def _conv3(stage, cw, halo, start, rows):
    at = lambda shift: stage[halo + start - shift:halo + start - shift + rows, :]
    return cw[0:1, :] * at(2) + cw[1:2, :] * at(1) + cw[2:3, :] * at(0)


def _sgu_kernel(x_ref, wu_ref, wv_ref, g_ref, b_ref, ws_ref, bs_ref, o_ref, xs_ref):
    tm, tn = o_ref.shape
    _stage_block(x_ref, xs_ref, pl.program_id(1) == 0)

    row = jax.lax.broadcasted_iota(jnp.int32, (CHUNK, CHUNK), 0)
    col = jax.lax.broadcasted_iota(jnp.int32, (CHUNK, CHUNK), 1)
    causal = row >= col
    heads = range(tn // HD)
    mix_w = [jnp.where(causal, ws_ref[h], 0.0).astype(BF16) for h in heads]

    def project(p):
        x = xs_ref[p * ROWS:(p + 1) * ROWS, :]
        return _dot(x, wu_ref[...]), _dot(x, wv_ref[...])

    def normalise(zv):
        v = _gelu_tanh(zv)
        out = []
        for h in heads:
            hs = slice(h * HD, (h + 1) * HD)
            vh = v[:, hs]
            mu = jnp.mean(vh, axis=-1, keepdims=True)
            dv = vh - mu
            var = jnp.mean(dv * dv, axis=-1, keepdims=True)
            out.append((dv * jax.lax.rsqrt(var + LN_EPS) * g_ref[:, hs] + b_ref[:, hs]).astype(BF16))
        return out

    def mix(p, zu, vn):
        u = _gelu_tanh(zu)
        for h in heads:
            hs = slice(h * HD, (h + 1) * HD)
            for c in range(ROWS // CHUNK):
                cs = slice(c * CHUNK, (c + 1) * CHUNK)
                mixed = _dot(mix_w[h], vn[h][cs, :]) + bs_ref[h]
                o_ref[p * ROWS + c * CHUNK:p * ROWS + (c + 1) * CHUNK, hs] = (
                    u[cs, hs] * mixed).astype(o_ref.dtype)

    zu, zv = project(0)
    vn = normalise(zv)
    for p in range(1, tm // ROWS):
        zu_next, zv_next = project(p)
        mix(p - 1, zu, vn)
        vn = normalise(zv_next)
        zu = zu_next
    mix(tm // ROWS - 1, zu, vn)


def _sgu_branch(xb, w_in, ln_g, ln_b, ws, bs_wide, layer):
    t = xb.shape[0]
    tn = TN_FRONT
    return pl.pallas_call(
        _sgu_kernel,
        grid=(t // TM, D_A // tn),
        in_specs=[
            pl.BlockSpec((TM, D_MODEL), lambda i, j: (i, 0)),
            _col_tile(D_MODEL, tn, lambda i, j: OFF_A // tn + j),
            _col_tile(D_MODEL, tn, lambda i, j: (OFF_A + D_A) // tn + j),
            pl.BlockSpec((None, 1, tn), lambda i, j: (layer, 0, j)),
            pl.BlockSpec((None, 1, tn), lambda i, j: (layer, 0, j)),
            pl.BlockSpec((None, tn // HD, CHUNK, CHUNK), lambda i, j: (layer, j, 0, 0)),
            pl.BlockSpec((None, tn // HD, CHUNK, HD), lambda i, j: (layer, j, 0, 0)),
        ],
        out_specs=pl.BlockSpec((TM, tn), lambda i, j: (i, j)),
        out_shape=jax.ShapeDtypeStruct((t, D_A), BF16),
        scratch_shapes=[pltpu.VMEM((TM, D_MODEL), BF16)],
        compiler_params=_params("parallel", "arbitrary"),
        name="sgu_branch",
    )(xb, w_in, w_in, ln_g, ln_b, ws, bs_wide)


def _sconv_kernel(x_ref, wb_ref, wc_ref, wh_ref, cw_ref, o_ref, xs_ref, p_ref, carry_ref,
                  *, blocks_per_seq):
    i, j = pl.program_id(0), pl.program_id(1)
    tm = o_ref.shape[0]
    _stage_block(x_ref, xs_ref, j == 0)
    _init_halo(p_ref.at[0:HALO_SHORT], carry_ref.at[j], i % blocks_per_seq == 0)

    for r in range(tm // ROWS):
        x = xs_ref[r * ROWS:(r + 1) * ROWS, :]
        p_ref[HALO_SHORT + r * ROWS:HALO_SHORT + (r + 1) * ROWS, :] = (
            _dot(x, wc_ref[...]) * _dot(x, wh_ref[...]))
        y = _conv3(p_ref, cw_ref[...], HALO_SHORT, r * ROWS, ROWS)
        o_ref[r * ROWS:(r + 1) * ROWS, :] = (_dot(x, wb_ref[...]) * y).astype(o_ref.dtype)
    carry_ref[j] = p_ref[tm:tm + HALO_SHORT, :]


def _sconv_branch(xb, w_in, b_conv, layer, blocks_per_seq):
    t = xb.shape[0]
    tn = TN_FRONT
    nj = D_B // tn
    wspec = lambda off: _col_tile(D_MODEL, tn, lambda i, j: off // tn + j)
    return pl.pallas_call(
        functools.partial(_sconv_kernel, blocks_per_seq=blocks_per_seq),
        grid=(t // TM, nj),
        in_specs=[
            pl.BlockSpec((TM, D_MODEL), lambda i, j: (i, 0)),
            wspec(OFF_B), wspec(OFF_B + D_B), wspec(OFF_B + 2 * D_B),
            pl.BlockSpec((None, SC_K, tn), lambda i, j: (layer, 0, j)),
        ],
        out_specs=pl.BlockSpec((TM, tn), lambda i, j: (i, j)),
        out_shape=jax.ShapeDtypeStruct((t, D_B), BF16),
        scratch_shapes=[pltpu.VMEM((TM, D_MODEL), BF16),
                        pltpu.VMEM((HALO_SHORT + TM, tn), F32),
                        pltpu.VMEM((nj, HALO_SHORT, tn), F32)],
        compiler_params=_params("arbitrary", "arbitrary"),
        name="sconv_branch",
    )(xb, w_in, w_in, w_in, b_conv)


def _cconv_kernel(x_ref, wa_ref, wg_ref, cw_ref, cb_ref, o_ref, xs_ref, ca_ref, carry_ref,
                  *, blocks_per_seq):
    i, j = pl.program_id(0), pl.program_id(1)
    nslab, tm, _ = o_ref.shape
    _stage_block(x_ref, xs_ref, j == 0)
    _init_halo(ca_ref.at[:, 0:HALO_LONG], carry_ref.at[j], i % blocks_per_seq == 0)

    first = HALO_LONG - (CF_K - 1)

    def conv_group(s, q):
        ls = slice(s * V7X_LANES, (s + 1) * V7X_LANES)
        accs = [jnp.broadcast_to(cb_ref[:, ls], (V7X_SUBLANES, V7X_LANES))] * ROW_STRIDE
        for k in range(CF_K):
            wk = jnp.broadcast_to(cw_ref[k:k + 1, ls], (V7X_SUBLANES, V7X_LANES))
            for r in range(ROW_STRIDE):
                rows = pl.ds(q * ROW_GROUP + r + first + k, V7X_SUBLANES, stride=ROW_STRIDE)
                accs[r] = accs[r] + wk * ca_ref[s, rows, :]
        for r in range(ROW_STRIDE):
            o_ref[s, pl.ds(q * ROW_GROUP + r, V7X_SUBLANES, stride=ROW_STRIDE), :] = accs[r]

    groups_per_part = ROWS // ROW_GROUP

    for part in range(tm // ROWS):
        x = xs_ref[part * ROWS:(part + 1) * ROWS, :]
        ca = _dot(x, wa_ref[...]) * _sigmoid(_dot(x, wg_ref[...]))
        data = slice(HALO_LONG + part * ROWS, HALO_LONG + (part + 1) * ROWS)
        for s in range(nslab):
            ca_ref[s, data, :] = ca[:, s * V7X_LANES:(s + 1) * V7X_LANES]
        for s in range(nslab):
            for q in range(part * groups_per_part, (part + 1) * groups_per_part):
                conv_group(s, q)
    carry_ref[j] = ca_ref[:, tm:tm + HALO_LONG, :]


def _cconv_branch(xb, w_in, c_conv, c_conv_b, layer, blocks_per_seq):
    t = xb.shape[0]
    tn = TN_FRONT
    nj = D_C // tn
    nslab = tn // V7X_LANES
    wspec = lambda off: _col_tile(D_MODEL, tn, lambda i, j: off // tn + j)
    return pl.pallas_call(
        functools.partial(_cconv_kernel, blocks_per_seq=blocks_per_seq),
        grid=(t // TM, nj),
        in_specs=[
            pl.BlockSpec((TM, D_MODEL), lambda i, j: (i, 0)),
            wspec(OFF_C), wspec(OFF_C + D_C),
            pl.BlockSpec((None, CF_K, tn), lambda i, j: (layer, 0, j)),
            pl.BlockSpec((None, 1, tn), lambda i, j: (layer, 0, j)),
        ],
        out_specs=pl.BlockSpec((nslab, TM, V7X_LANES), lambda i, j: (j, i, 0)),
        out_shape=jax.ShapeDtypeStruct((D_C // V7X_LANES, t, V7X_LANES), F32),
        scratch_shapes=[pltpu.VMEM((TM, D_MODEL), BF16),
                        pltpu.VMEM((nslab, HALO_LONG + TM, V7X_LANES), F32),
                        pltpu.VMEM((nj, nslab, HALO_LONG, V7X_LANES), F32)],
        compiler_params=_params("arbitrary", "arbitrary"),
        name="cconv_branch",
    )(xb, w_in, w_in, c_conv, c_conv_b)


def _ln_silu_kernel(v_ref, g_ref, b_ref, o_ref):
    nslab, _, lanes = v_ref.shape
    width = nslab * lanes
    v = v_ref[...]
    mu = jnp.sum(jnp.sum(v, axis=0), axis=-1, keepdims=True) / width
    dv = v - mu[None]
    var = jnp.sum(jnp.sum(dv * dv, axis=0), axis=-1, keepdims=True) / width
    y = dv * jax.lax.rsqrt(var + LN_EPS)[None] * g_ref[...] + b_ref[...]
    y = y * _sigmoid(y)
    for s in range(nslab):
        o_ref[:, s * lanes:(s + 1) * lanes] = y[s].astype(o_ref.dtype)


def _ln_silu(v, g, b, layer):
    nslab, t, lanes = v.shape
    par_spec = pl.BlockSpec((None, nslab, 1, lanes), lambda i: (layer, 0, 0, 0))
    return pl.pallas_call(
        _ln_silu_kernel,
        grid=(t // TR_LN,),
        in_specs=[pl.BlockSpec((nslab, TR_LN, lanes), lambda i: (0, i, 0)), par_spec, par_spec],
        out_specs=pl.BlockSpec((TR_LN, nslab * lanes), lambda i: (i, 0)),
        out_shape=jax.ShapeDtypeStruct((t, nslab * lanes), BF16),
        compiler_params=_params("parallel"),
        name="ln_silu",
    )(v, g, b)


def _ln_rows_kernel(v_ref, g_ref, b_ref, o_ref, ob_ref):
    v = v_ref[...]
    mu = jnp.mean(v, axis=-1, keepdims=True)
    dv = v - mu
    var = jnp.mean(dv * dv, axis=-1, keepdims=True)
    y = dv * jax.lax.rsqrt(var + LN_EPS) * g_ref[...] + b_ref[...]
    o_ref[...] = y
    ob_ref[...] = y.astype(ob_ref.dtype)


def _ln_rows(v, g, b, layer):
    t, c = v.shape
    row_spec = pl.BlockSpec((TR_LN, c), lambda i: (i, 0))
    par_spec = pl.BlockSpec((None, 1, c), lambda i: (layer, 0, 0))
    return pl.pallas_call(
        _ln_rows_kernel,
        grid=(t // TR_LN,),
        in_specs=[row_spec, par_spec, par_spec],
        out_specs=[row_spec, row_spec],
        out_shape=[jax.ShapeDtypeStruct((t, c), F32), jax.ShapeDtypeStruct((t, c), BF16)],
        compiler_params=_params("parallel"),
        name="ln_rows",
    )(v, g, b)


def _merge_kernel(x_ref, a_ref, b_ref, c_ref, wg0_ref, wg1_ref, wg2_ref, wa_ref, wb_ref, wc_ref,
                  bg_ref, o_ref):
    x = x_ref[...]
    g0 = _sigmoid(_dot(x, wg0_ref[...]) + bg_ref[0:1, :])
    m = g0 * _dot(a_ref[...], wa_ref[...])
    g1 = _sigmoid(_dot(x, wg1_ref[...]) + bg_ref[1:2, :])
    m = m + g1 * _dot(b_ref[...], wb_ref[...])
    g2 = _sigmoid(_dot(x, wg2_ref[...]) + bg_ref[2:3, :])
    m = m + g2 * _dot(c_ref[...], wc_ref[...])
    o_ref[...] = m.astype(o_ref.dtype)


def _merge(xb, act_a, act_b, act_c, w_in, a_out, b_out, c_out, b_gate, layer):
    t = xb.shape[0]
    tn = TN_MERGE
    resident = dict(pipeline_mode=pl.Buffered(1))
    gspec = lambda k: _col_tile(D_MODEL, tn, lambda i, j: (OFF_G + k * D_MODEL) // tn + j)
    ospec = lambda width: _col_tile(width, tn, lambda i, j: j)
    return pl.pallas_call(
        _merge_kernel,
        grid=(t // TM, D_MODEL // tn),
        in_specs=[
            pl.BlockSpec((TM, D_MODEL), lambda i, j: (i, 0), **resident),
            pl.BlockSpec((TM, D_A), lambda i, j: (i, 0), **resident),
            pl.BlockSpec((TM, D_B), lambda i, j: (i, 0), **resident),
            pl.BlockSpec((TM, D_C), lambda i, j: (i, 0), **resident),
            gspec(0), gspec(1), gspec(2),
            ospec(D_A), ospec(D_B), ospec(D_C),
            pl.BlockSpec((None, N_BRANCH, tn), lambda i, j: (layer, 0, j)),
        ],
        out_specs=pl.BlockSpec((TM, tn), lambda i, j: (i, j)),
        out_shape=jax.ShapeDtypeStruct((t, D_MODEL), BF16),
        compiler_params=_params("parallel", "arbitrary"),
        name="gate_merge",
    )(xb, act_a, act_b, act_c, w_in, w_in, w_in, a_out, b_out, c_out, b_gate)


def _proj_residual_kernel(a_ref, w_ref, x_ref, o_ref, *, alpha):
    half = a_ref.shape[0] // 2
    for rows in (slice(0, half), slice(half, 2 * half)):
        o_ref[rows, :] = alpha * x_ref[rows, :] + _dot(a_ref[rows, :], w_ref[...])


def _proj_residual(a, w, x, alpha, tn, name):
    t, k = a.shape
    return pl.pallas_call(
        functools.partial(_proj_residual_kernel, alpha=alpha),
        grid=(t // TM, D_MODEL // tn),
        in_specs=[pl.BlockSpec((TM, k), lambda i, j: (i, 0), pipeline_mode=pl.Buffered(1)),
                  _col_tile(k, tn, lambda i, j: j),
                  pl.BlockSpec((TM, tn), lambda i, j: (i, j))],
        out_specs=pl.BlockSpec((TM, tn), lambda i, j: (i, j)),
        out_shape=jax.ShapeDtypeStruct((t, D_MODEL), F32),
        compiler_params=_params("parallel", "arbitrary"),
        name=name,
    )(a, w, x)


def _ffn_up_kernel(*refs, n_cast, blocks_per_seq):
    x_ref, w1_ref, w2_ref, cw1_ref, cw2_ref = refs[:5]
    cast_src = refs[5:5 + n_cast]
    o_ref = refs[5 + n_cast]
    cast_dst = refs[6 + n_cast:6 + 2 * n_cast]
    xs_ref, h1_ref, h2_ref, carry_ref = refs[6 + 2 * n_cast:]

    i, j = pl.program_id(0), pl.program_id(1)
    tm = o_ref.shape[0]
    _stage_block(x_ref, xs_ref, j == 0)
    seq_start = i % blocks_per_seq == 0
    _init_halo(h1_ref.at[0:HALO_SHORT], carry_ref.at[j, 0], seq_start)
    _init_halo(h2_ref.at[0:HALO_SHORT], carry_ref.at[j, 1], seq_start)

    nparts = tm // ROWS
    for r in range(nparts):
        x = xs_ref[r * ROWS:(r + 1) * ROWS, :]
        data = slice(HALO_SHORT + r * ROWS, HALO_SHORT + (r + 1) * ROWS)
        h1_ref[data, :] = _dot(x, w1_ref[...])
        h2_ref[data, :] = _dot(x, w2_ref[...])
        c1 = _conv3(h1_ref, cw1_ref[...], HALO_SHORT, r * ROWS, ROWS)
        c2 = _conv3(h2_ref, cw2_ref[...], HALO_SHORT, r * ROWS, ROWS)
        o_ref[r * ROWS:(r + 1) * ROWS, :] = (c1 * _sigmoid(c1) * c2).astype(o_ref.dtype)
        _cast_part(cast_src, cast_dst, r, nparts)
    carry_ref[j, 0] = h1_ref[tm:tm + HALO_SHORT, :]
    carry_ref[j, 1] = h2_ref[tm:tm + HALO_SHORT, :]


def _ffn_up(xb, f_up, f_conv, layer, blocks_per_seq, cast_weights=(), cast_layer=0):
    t = xb.shape[0]
    nj = D_FF // TF
    ni = t // TM
    step_of = lambda i, j: i * nj + j
    cast_in, cast_out, cast_shapes = _cast_jobs(cast_weights, cast_layer, ni * nj)
    outs = pl.pallas_call(
        functools.partial(_ffn_up_kernel, n_cast=len(cast_weights), blocks_per_seq=blocks_per_seq),
        grid=(ni, nj),
        in_specs=[
            pl.BlockSpec((TM, D_MODEL), lambda i, j: (i, 0)),
            _col_tile(D_MODEL, TF, lambda i, j: j),
            _col_tile(D_MODEL, TF, lambda i, j: nj + j),
            pl.BlockSpec((None, FF_K, TF), lambda i, j: (layer, 0, j)),
            pl.BlockSpec((None, FF_K, TF), lambda i, j: (layer, 0, nj + j)),
        ] + [spec(step_of) for spec in cast_in],
        out_specs=[pl.BlockSpec((TM, TF), lambda i, j: (i, j))] + [spec(step_of) for spec in cast_out],
        out_shape=[jax.ShapeDtypeStruct((t, D_FF), BF16)] + cast_shapes,
        scratch_shapes=[pltpu.VMEM((TM, D_MODEL), BF16),
                        pltpu.VMEM((HALO_SHORT + TM, TF), F32),
                        pltpu.VMEM((HALO_SHORT + TM, TF), F32),
                        pltpu.VMEM((nj, 2, HALO_SHORT, TF), F32)],
        compiler_params=_params("arbitrary", "arbitrary"),
        name="ffn_up",
    )(xb, f_up, f_up, f_conv, f_conv, *cast_weights)
    return outs[0], outs[1:]


def kernel(x, w_in, b_gate, a_ln_g, a_ln_b, a_ws, a_bs, a_out, b_conv, b_out, c_conv, c_conv_b,
           c_ln_g, c_ln_b, c_out, w_o, ln1_g, ln1_b, f_up, f_conv, f_down, ln2_g, ln2_b):
    bsz, seq, d = x.shape
    depth = w_in.shape[0]
    assert d == D_MODEL and seq % TM == 0 and w_in.shape[-1] == N_IN and f_down.shape[1] == D_FF
    alpha = (2.0 * depth) ** 0.25
    blocks_per_seq = seq // TM
    t = bsz * seq

    weights = (w_in, a_out, b_out, c_out, w_o, f_up, f_down)
    weights_b = [w[0].astype(BF16) for w in weights]
    row3 = lambda p: p[:, None, :]
    a_ln_g3, a_ln_b3, c_conv_b3 = map(row3, (a_ln_g, a_ln_b, c_conv_b))
    ln1_g3, ln1_b3, ln2_g3, ln2_b3 = map(row3, (ln1_g, ln1_b, ln2_g, ln2_b))
    slabs = lambda p: p.reshape(depth, D_C // V7X_LANES, 1, V7X_LANES)
    c_ln_g4, c_ln_b4 = slabs(c_ln_g), slabs(c_ln_b)
    bs_wide = jnp.broadcast_to(a_bs[..., None], a_bs.shape + (HD,))

    xf = x.reshape(t, d)
    xb = xf.astype(BF16)
    for l in range(depth):
        w_in_b, a_out_b, b_out_b, c_out_b, w_o_b, f_up_b, f_down_b = weights_b
        act_a = _sgu_branch(xb, w_in_b, a_ln_g3, a_ln_b3, a_ws, bs_wide, l)
        act_b = _sconv_branch(xb, w_in_b, b_conv, l, blocks_per_seq)
        pre_c = _cconv_branch(xb, w_in_b, c_conv, c_conv_b3, l, blocks_per_seq)
        act_c = _ln_silu(pre_c, c_ln_g4, c_ln_b4, l)
        m = _merge(xb, act_a, act_b, act_c, w_in_b, a_out_b, b_out_b, c_out_b, b_gate, l)
        mixed = _proj_residual(m, w_o_b, xf, alpha, TN_OUT, "out_proj")
        xf, xb = _ln_rows(mixed, ln1_g3, ln1_b3, l)
        next_weights = weights if l + 1 < depth else ()
        act, weights_b = _ffn_up(xb, f_up_b, f_conv, l, blocks_per_seq, next_weights, l + 1)
        ffn = _proj_residual(act, f_down_b, xf, alpha, TN_DOWN, "ffn_down")
        xf, xb = _ln_rows(ffn, ln2_g3, ln2_b3, l)
    return xf.reshape(bsz, seq, d)
```

```python
import functools
import math

import jax
import jax.numpy as jnp
from jax.experimental import pallas as pl
from jax.experimental.pallas import tpu as pltpu

F32 = jnp.float32
BF16 = jnp.bfloat16

D_MODEL = 4096
CHUNK = 128
HD = 128
D_A = D_MODEL // 2
H_A = D_A // HD
D_B = D_MODEL // 2
SC_K = 3
D_C = D_MODEL // 2
CF_K = 31
D_FF = 11008
FF_K = 3
N_BRANCH = 3
LN_EPS = 1e-5
OFF_A = 0
OFF_B = OFF_A + 2 * D_A
OFF_C = OFF_B + 3 * D_B
OFF_G = OFF_C + 2 * D_C
N_IN = OFF_G + N_BRANCH * D_MODEL

V7X_SUBLANES = 8
V7X_LANES = 128
BF16_ROWS = 2 * V7X_SUBLANES
V7X_VMEM_BYTES = 64 << 20

TM = 1024
ROWS = 256
TN_FRONT = 256
TN_MERGE = 256
TN_OUT = 512
TF = 256
TN_DOWN = 256
TR_LN = 256
HALO_SHORT = V7X_SUBLANES
HALO_LONG = 32
ROW_STRIDE = 4
ROW_GROUP = ROW_STRIDE * V7X_SUBLANES
VMEM_LIMIT = V7X_VMEM_BYTES - (8 << 20)

assert ROWS % CHUNK == 0 and TM % ROWS == 0 and ROWS % ROW_GROUP == 0


def _sigmoid(v):
    return 0.5 * (1.0 + jnp.tanh(0.5 * v))


def _gelu_tanh(v):
    c = math.sqrt(2.0 / math.pi)
    return 0.5 * v * (1.0 + jnp.tanh(c * (v + 0.044715 * (v * v * v))))


def _dot(a, b):
    return jnp.dot(a, b, preferred_element_type=F32)


def _params(*sem):
    return pltpu.CompilerParams(dimension_semantics=sem, vmem_limit_bytes=VMEM_LIMIT)


def _col_tile(k, tn, tile_of):
    return pl.BlockSpec((k, tn), lambda *g: (0, tile_of(*g)))


def _cast_jobs(weights, layer, steps):
    in_specs, out_specs, out_shapes = [], [], []
    for w in weights:
        _, r, c = w.shape
        rows = next(m for m in range(BF16_ROWS, r + 1, BF16_ROWS) if r % m == 0 and r // m <= steps)
        last = r // rows - 1
        in_specs.append(lambda step_of, rows=rows, c=c, last=last: pl.BlockSpec(
            (None, rows, c), lambda *g: (layer, jnp.minimum(step_of(*g), last), 0)))
        out_specs.append(lambda step_of, rows=rows, c=c, last=last: pl.BlockSpec(
            (rows, c), lambda *g: (jnp.minimum(step_of(*g), last), 0)))
        out_shapes.append(jax.ShapeDtypeStruct((r, c), BF16))
    return in_specs, out_specs, out_shapes


def _split_refs(refs, n_in, n_out, n_cast):
    a, b, c = n_in + n_cast, n_in + n_cast + n_out, n_in + 2 * n_cast + n_out
    return refs[:n_in], refs[a:b], refs[c:], refs[n_in:a], refs[b:c]


def _cast_part(src_refs, dst_refs, part, nparts):
    for src, dst in zip(src_refs, dst_refs):
        c = src.shape[-1] // nparts
        dst[:, part * c:(part + 1) * c] = src[:, part * c:(part + 1) * c].astype(dst.dtype)


def _call_with_casts(body, args, in_specs, out_specs, out_shapes, *, grid, scratch_shapes,
                     semantics, name, cast_weights=(), cast_layer=0):
    n_in, n_out, n_cast = len(in_specs), len(out_specs), len(cast_weights)

    def step_of(*g):
        step = 0
        for index, extent in zip(g, grid):
            step = step * extent + index
        return step

    cast_in, cast_out, cast_shapes = _cast_jobs(cast_weights, cast_layer, math.prod(grid))

    def kernel(*refs):
        ins, outs, scratch, src, dst = _split_refs(refs, n_in, n_out, n_cast)
        body(*ins, *outs, *scratch, cast=functools.partial(_cast_part, src, dst))

    results = pl.pallas_call(
        kernel,
        grid=grid,
        in_specs=list(in_specs) + [spec(step_of) for spec in cast_in],
        out_specs=list(out_specs) + [spec(step_of) for spec in cast_out],
        out_shape=list(out_shapes) + cast_shapes,
        scratch_shapes=scratch_shapes,
        compiler_params=_params(*semantics),
        name=name,
    )(*args, *cast_weights)
    return results[:n_out], results[n_out:]


def _stage_block(x_ref, xs_ref, first_use):
    @pl.when(first_use)
    def _():
        xs_ref[...] = x_ref[...]


def _init_halo(stage_halo, carry, seq_start):
    @pl.when(seq_start)
    def _():
        stage_halo[...] = jnp.zeros(stage_halo.shape, F32)

    @pl.when(jnp.logical_not(seq_start))
    def _():
        stage_halo[...] = carry[...]


def _conv3(stage, cw, halo, start, rows):
    at = lambda shift: stage[halo + start - shift:halo + start - shift + rows, :]
    return cw[0:1, :] * at(2) + cw[1:2, :] * at(1) + cw[2:3, :] * at(0)


def _sgu_kernel(x_ref, wu_ref, wv_ref, g_ref, b_ref, ws_ref, bs_ref, o_ref, xs_ref, *, cast):
    tm, tn = o_ref.shape
    _stage_block(x_ref, xs_ref, pl.program_id(1) == 0)

    row = jax.lax.broadcasted_iota(jnp.int32, (CHUNK, CHUNK), 0)
    col = jax.lax.broadcasted_iota(jnp.int32, (CHUNK, CHUNK), 1)
    causal = row >= col
    heads = range(tn // HD)
    mix_w = [jnp.where(causal, ws_ref[h], 0.0).astype(BF16) for h in heads]

    def project(p):
        x = xs_ref[p * ROWS:(p + 1) * ROWS, :]
        return _dot(x, wu_ref[...]), _dot(x, wv_ref[...])

    def normalise(zv):
        v = _gelu_tanh(zv)
        out = []
        for h in heads:
            hs = slice(h * HD, (h + 1) * HD)
            vh = v[:, hs]
            mu = jnp.mean(vh, axis=-1, keepdims=True)
            dv = vh - mu
            var = jnp.mean(dv * dv, axis=-1, keepdims=True)
            out.append((dv * jax.lax.rsqrt(var + LN_EPS) * g_ref[:, hs] + b_ref[:, hs]).astype(BF16))
        return out

    def mix(p, zu, vn):
        u = _gelu_tanh(zu)
        for h in heads:
            hs = slice(h * HD, (h + 1) * HD)
            for c in range(ROWS // CHUNK):
                cs = slice(c * CHUNK, (c + 1) * CHUNK)
                mixed = _dot(mix_w[h], vn[h][cs, :]) + bs_ref[h]
                o_ref[p * ROWS + c * CHUNK:p * ROWS + (c + 1) * CHUNK, hs] = (
                    u[cs, hs] * mixed).astype(o_ref.dtype)

    nparts = tm // ROWS
    zu, zv = project(0)
    vn = normalise(zv)
    cast(0, nparts)
    for p in range(1, nparts):
        zu_next, zv_next = project(p)
        mix(p - 1, zu, vn)
        vn = normalise(zv_next)
        zu = zu_next
        cast(p, nparts)
    mix(nparts - 1, zu, vn)


def _sgu_branch(xb, w_in, ln_g, ln_b, ws, bs_wide, layer, **casts):
    t = xb.shape[0]
    tn = TN_FRONT
    (act,), cast_out = _call_with_casts(
        _sgu_kernel, (xb, w_in, w_in, ln_g, ln_b, ws, bs_wide),
        in_specs=[
            pl.BlockSpec((TM, D_MODEL), lambda i, j: (i, 0)),
            _col_tile(D_MODEL, tn, lambda i, j: OFF_A // tn + j),
            _col_tile(D_MODEL, tn, lambda i, j: (OFF_A + D_A) // tn + j),
            pl.BlockSpec((None, 1, tn), lambda i, j: (layer, 0, j)),
            pl.BlockSpec((None, 1, tn), lambda i, j: (layer, 0, j)),
            pl.BlockSpec((None, tn // HD, CHUNK, CHUNK), lambda i, j: (layer, j, 0, 0)),
            pl.BlockSpec((None, tn // HD, CHUNK, HD), lambda i, j: (layer, j, 0, 0)),
        ],
        out_specs=[pl.BlockSpec((TM, tn), lambda i, j: (i, j))],
        out_shapes=[jax.ShapeDtypeStruct((t, D_A), BF16)],
        grid=(t // TM, D_A // tn),
        scratch_shapes=[pltpu.VMEM((TM, D_MODEL), BF16)],
        semantics=("arbitrary", "arbitrary"),
        name="sgu_branch", **casts)
    return act, cast_out


def _sconv_kernel(x_ref, wb_ref, wc_ref, wh_ref, cw_ref, o_ref, xs_ref, p_ref, carry_ref,
                  *, blocks_per_seq, cast):
    i, j = pl.program_id(0), pl.program_id(1)
    tm = o_ref.shape[0]
    _stage_block(x_ref, xs_ref, j == 0)
    _init_halo(p_ref.at[0:HALO_SHORT], carry_ref.at[j], i % blocks_per_seq == 0)

    nparts = tm // ROWS
    for r in range(nparts):
        x = xs_ref[r * ROWS:(r + 1) * ROWS, :]
        p_ref[HALO_SHORT + r * ROWS:HALO_SHORT + (r + 1) * ROWS, :] = (
            _dot(x, wc_ref[...]) * _dot(x, wh_ref[...]))
        y = _conv3(p_ref, cw_ref[...], HALO_SHORT, r * ROWS, ROWS)
        o_ref[r * ROWS:(r + 1) * ROWS, :] = (_dot(x, wb_ref[...]) * y).astype(o_ref.dtype)
        cast(r, nparts)
    carry_ref[j] = p_ref[tm:tm + HALO_SHORT, :]


def _sconv_branch(xb, w_in, b_conv, layer, blocks_per_seq, **casts):
    t = xb.shape[0]
    tn = TN_FRONT
    nj = D_B // tn
    wspec = lambda off: _col_tile(D_MODEL, tn, lambda i, j: off // tn + j)
    (act,), cast_out = _call_with_casts(
        functools.partial(_sconv_kernel, blocks_per_seq=blocks_per_seq),
        (xb, w_in, w_in, w_in, b_conv),
        in_specs=[
            pl.BlockSpec((TM, D_MODEL), lambda i, j: (i, 0)),
            wspec(OFF_B), wspec(OFF_B + D_B), wspec(OFF_B + 2 * D_B),
            pl.BlockSpec((None, SC_K, tn), lambda i, j: (layer, 0, j)),
        ],
        out_specs=[pl.BlockSpec((TM, tn), lambda i, j: (i, j))],
        out_shapes=[jax.ShapeDtypeStruct((t, D_B), BF16)],
        grid=(t // TM, nj),
        scratch_shapes=[pltpu.VMEM((TM, D_MODEL), BF16),
                        pltpu.VMEM((HALO_SHORT + TM, tn), F32),
                        pltpu.VMEM((nj, HALO_SHORT, tn), F32)],
        semantics=("arbitrary", "arbitrary"),
        name="sconv_branch", **casts)
    return act, cast_out


def _cconv_kernel(x_ref, wa_ref, wg_ref, cw_ref, cb_ref, o_ref, xs_ref, ca_ref, carry_ref,
                  *, blocks_per_seq, cast):
    i, j = pl.program_id(0), pl.program_id(1)
    nslab, tm, _ = o_ref.shape
    _stage_block(x_ref, xs_ref, j == 0)
    _init_halo(ca_ref.at[:, 0:HALO_LONG], carry_ref.at[j], i % blocks_per_seq == 0)

    first = HALO_LONG - (CF_K - 1)

    def conv_group(s, q):
        ls = slice(s * V7X_LANES, (s + 1) * V7X_LANES)
        accs = [jnp.broadcast_to(cb_ref[:, ls], (V7X_SUBLANES, V7X_LANES))] * ROW_STRIDE
        for k in range(CF_K):
            wk = jnp.broadcast_to(cw_ref[k:k + 1, ls], (V7X_SUBLANES, V7X_LANES))
            for r in range(ROW_STRIDE):
                rows = pl.ds(q * ROW_GROUP + r + first + k, V7X_SUBLANES, stride=ROW_STRIDE)
                accs[r] = accs[r] + wk * ca_ref[s, rows, :]
        for r in range(ROW_STRIDE):
            o_ref[s, pl.ds(q * ROW_GROUP + r, V7X_SUBLANES, stride=ROW_STRIDE), :] = accs[r]

    groups_per_part = ROWS // ROW_GROUP

    nparts = tm // ROWS
    for part in range(nparts):
        x = xs_ref[part * ROWS:(part + 1) * ROWS, :]
        ca = _dot(x, wa_ref[...]) * _sigmoid(_dot(x, wg_ref[...]))
        data = slice(HALO_LONG + part * ROWS, HALO_LONG + (part + 1) * ROWS)
        for s in range(nslab):
            ca_ref[s, data, :] = ca[:, s * V7X_LANES:(s + 1) * V7X_LANES]
        cast(part, nparts)
        for s in range(nslab):
            for q in range(part * groups_per_part, (part + 1) * groups_per_part):
                conv_group(s, q)
    carry_ref[j] = ca_ref[:, tm:tm + HALO_LONG, :]


def _cconv_branch(xb, w_in, c_conv, c_conv_b, layer, blocks_per_seq, **casts):
    t = xb.shape[0]
    tn = TN_FRONT
    nj = D_C // tn
    nslab = tn // V7X_LANES
    wspec = lambda off: _col_tile(D_MODEL, tn, lambda i, j: off // tn + j)
    (pre,), cast_out = _call_with_casts(
        functools.partial(_cconv_kernel, blocks_per_seq=blocks_per_seq),
        (xb, w_in, w_in, c_conv, c_conv_b),
        in_specs=[
            pl.BlockSpec((TM, D_MODEL), lambda i, j: (i, 0)),
            wspec(OFF_C), wspec(OFF_C + D_C),
            pl.BlockSpec((None, CF_K, tn), lambda i, j: (layer, 0, j)),
            pl.BlockSpec((None, 1, tn), lambda i, j: (layer, 0, j)),
        ],
        out_specs=[pl.BlockSpec((nslab, TM, V7X_LANES), lambda i, j: (j, i, 0))],
        out_shapes=[jax.ShapeDtypeStruct((D_C // V7X_LANES, t, V7X_LANES), F32)],
        grid=(t // TM, nj),
        scratch_shapes=[pltpu.VMEM((TM, D_MODEL), BF16),
                        pltpu.VMEM((nslab, HALO_LONG + TM, V7X_LANES), F32),
                        pltpu.VMEM((nj, nslab, HALO_LONG, V7X_LANES), F32)],
        semantics=("arbitrary", "arbitrary"),
        name="cconv_branch", **casts)
    return pre, cast_out


def _ln_silu_kernel(v_ref, g_ref, b_ref, o_ref):
    nslab, _, lanes = v_ref.shape
    width = nslab * lanes
    v = v_ref[...]
    mu = jnp.sum(jnp.sum(v, axis=0), axis=-1, keepdims=True) / width
    dv = v - mu[None]
    var = jnp.sum(jnp.sum(dv * dv, axis=0), axis=-1, keepdims=True) / width
    y = dv * jax.lax.rsqrt(var + LN_EPS)[None] * g_ref[...] + b_ref[...]
    y = y * _sigmoid(y)
    for s in range(nslab):
        o_ref[:, s * lanes:(s + 1) * lanes] = y[s].astype(o_ref.dtype)


def _ln_silu(v, g, b, layer):
    nslab, t, lanes = v.shape
    par_spec = pl.BlockSpec((None, nslab, 1, lanes), lambda i: (layer, 0, 0, 0))
    return pl.pallas_call(
        _ln_silu_kernel,
        grid=(t // TR_LN,),
        in_specs=[pl.BlockSpec((nslab, TR_LN, lanes), lambda i: (0, i, 0)), par_spec, par_spec],
        out_specs=pl.BlockSpec((TR_LN, nslab * lanes), lambda i: (i, 0)),
        out_shape=jax.ShapeDtypeStruct((t, nslab * lanes), BF16),
        compiler_params=_params("parallel"),
        name="ln_silu",
    )(v, g, b)


def _ln_rows_kernel(v_ref, g_ref, b_ref, o_ref, ob_ref):
    v = v_ref[...]
    mu = jnp.mean(v, axis=-1, keepdims=True)
    dv = v - mu
    var = jnp.mean(dv * dv, axis=-1, keepdims=True)
    y = dv * jax.lax.rsqrt(var + LN_EPS) * g_ref[...] + b_ref[...]
    o_ref[...] = y
    ob_ref[...] = y.astype(ob_ref.dtype)


def _ln_rows(v, g, b, layer):
    t, c = v.shape
    row_spec = pl.BlockSpec((TR_LN, c), lambda i: (i, 0))
    par_spec = pl.BlockSpec((None, 1, c), lambda i: (layer, 0, 0))
    return pl.pallas_call(
        _ln_rows_kernel,
        grid=(t // TR_LN,),
        in_specs=[row_spec, par_spec, par_spec],
        out_specs=[row_spec, row_spec],
        out_shape=[jax.ShapeDtypeStruct((t, c), F32), jax.ShapeDtypeStruct((t, c), BF16)],
        compiler_params=_params("parallel"),
        name="ln_rows",
    )(v, g, b)


def _merge_kernel(x_ref, a_ref, b_ref, c_ref, wg0_ref, wg1_ref, wg2_ref, wa_ref, wb_ref, wc_ref,
                  bg_ref, o_ref):
    x = x_ref[...]
    g0 = _sigmoid(_dot(x, wg0_ref[...]) + bg_ref[0:1, :])
    m = g0 * _dot(a_ref[...], wa_ref[...])
    g1 = _sigmoid(_dot(x, wg1_ref[...]) + bg_ref[1:2, :])
    m = m + g1 * _dot(b_ref[...], wb_ref[...])
    g2 = _sigmoid(_dot(x, wg2_ref[...]) + bg_ref[2:3, :])
    m = m + g2 * _dot(c_ref[...], wc_ref[...])
    o_ref[...] = m.astype(o_ref.dtype)


def _merge(xb, act_a, act_b, act_c, w_in, a_out, b_out, c_out, b_gate, layer):
    t = xb.shape[0]
    tn = TN_MERGE
    resident = dict(pipeline_mode=pl.Buffered(1))
    gspec = lambda k: _col_tile(D_MODEL, tn, lambda i, j: (OFF_G + k * D_MODEL) // tn + j)
    ospec = lambda width: _col_tile(width, tn, lambda i, j: j)
    return pl.pallas_call(
        _merge_kernel,
        grid=(t // TM, D_MODEL // tn),
        in_specs=[
            pl.BlockSpec((TM, D_MODEL), lambda i, j: (i, 0), **resident),
            pl.BlockSpec((TM, D_A), lambda i, j: (i, 0), **resident),
            pl.BlockSpec((TM, D_B), lambda i, j: (i, 0), **resident),
            pl.BlockSpec((TM, D_C), lambda i, j: (i, 0), **resident),
            gspec(0), gspec(1), gspec(2),
            ospec(D_A), ospec(D_B), ospec(D_C),
            pl.BlockSpec((None, N_BRANCH, tn), lambda i, j: (layer, 0, j)),
        ],
        out_specs=pl.BlockSpec((TM, tn), lambda i, j: (i, j)),
        out_shape=jax.ShapeDtypeStruct((t, D_MODEL), BF16),
        compiler_params=_params("parallel", "arbitrary"),
        name="gate_merge",
    )(xb, act_a, act_b, act_c, w_in, w_in, w_in, a_out, b_out, c_out, b_gate)


def _proj_residual_kernel(a_ref, w_ref, x_ref, o_ref, *, alpha):
    half = a_ref.shape[0] // 2
    for rows in (slice(0, half), slice(half, 2 * half)):
        o_ref[rows, :] = alpha * x_ref[rows, :] + _dot(a_ref[rows, :], w_ref[...])


def _proj_residual(a, w, x, alpha, tn, name):
    t, k = a.shape
    return pl.pallas_call(
        functools.partial(_proj_residual_kernel, alpha=alpha),
        grid=(t // TM, D_MODEL // tn),
        in_specs=[pl.BlockSpec((TM, k), lambda i, j: (i, 0),
                               pipeline_mode=pl.Buffered(1) if k > D_MODEL else None),
                  _col_tile(k, tn, lambda i, j: j),
                  pl.BlockSpec((TM, tn), lambda i, j: (i, j))],
        out_specs=pl.BlockSpec((TM, tn), lambda i, j: (i, j)),
        out_shape=jax.ShapeDtypeStruct((t, D_MODEL), F32),
        compiler_params=_params("parallel", "arbitrary"),
        name=name,
    )(a, w, x)


def _ffn_up_kernel(x_ref, w1_ref, w2_ref, cw1_ref, cw2_ref, o_ref, xs_ref, h1_ref, h2_ref,
                   carry_ref, *, blocks_per_seq, cast):
    i, j = pl.program_id(0), pl.program_id(1)
    tm = o_ref.shape[0]
    _stage_block(x_ref, xs_ref, j == 0)
    seq_start = i % blocks_per_seq == 0
    _init_halo(h1_ref.at[0:HALO_SHORT], carry_ref.at[j, 0], seq_start)
    _init_halo(h2_ref.at[0:HALO_SHORT], carry_ref.at[j, 1], seq_start)

    nparts = tm // ROWS
    for r in range(nparts):
        x = xs_ref[r * ROWS:(r + 1) * ROWS, :]
        data = slice(HALO_SHORT + r * ROWS, HALO_SHORT + (r + 1) * ROWS)
        h1_ref[data, :] = _dot(x, w1_ref[...])
        h2_ref[data, :] = _dot(x, w2_ref[...])
        c1 = _conv3(h1_ref, cw1_ref[...], HALO_SHORT, r * ROWS, ROWS)
        c2 = _conv3(h2_ref, cw2_ref[...], HALO_SHORT, r * ROWS, ROWS)
        o_ref[r * ROWS:(r + 1) * ROWS, :] = (c1 * _sigmoid(c1) * c2).astype(o_ref.dtype)
        cast(r, nparts)
    carry_ref[j, 0] = h1_ref[tm:tm + HALO_SHORT, :]
    carry_ref[j, 1] = h2_ref[tm:tm + HALO_SHORT, :]


def _ffn_up(xb, f_up, f_conv, layer, blocks_per_seq, **casts):
    t = xb.shape[0]
    nj = D_FF // TF
    (act,), cast_out = _call_with_casts(
        functools.partial(_ffn_up_kernel, blocks_per_seq=blocks_per_seq),
        (xb, f_up, f_up, f_conv, f_conv),
        in_specs=[
            pl.BlockSpec((TM, D_MODEL), lambda i, j: (i, 0)),
            _col_tile(D_MODEL, TF, lambda i, j: j),
            _col_tile(D_MODEL, TF, lambda i, j: nj + j),
            pl.BlockSpec((None, FF_K, TF), lambda i, j: (layer, 0, j)),
            pl.BlockSpec((None, FF_K, TF), lambda i, j: (layer, 0, nj + j)),
        ],
        out_specs=[pl.BlockSpec((TM, TF), lambda i, j: (i, j))],
        out_shapes=[jax.ShapeDtypeStruct((t, D_FF), BF16)],
        grid=(t // TM, nj),
        scratch_shapes=[pltpu.VMEM((TM, D_MODEL), BF16),
                        pltpu.VMEM((HALO_SHORT + TM, TF), F32),
                        pltpu.VMEM((HALO_SHORT + TM, TF), F32),
                        pltpu.VMEM((nj, 2, HALO_SHORT, TF), F32)],
        semantics=("arbitrary", "arbitrary"),
        name="ffn_up", **casts)
    return act, cast_out


def kernel(x, w_in, b_gate, a_ln_g, a_ln_b, a_ws, a_bs, a_out, b_conv, b_out, c_conv, c_conv_b,
           c_ln_g, c_ln_b, c_out, w_o, ln1_g, ln1_b, f_up, f_conv, f_down, ln2_g, ln2_b):
    bsz, seq, d = x.shape
    depth = w_in.shape[0]
    assert d == D_MODEL and seq % TM == 0 and w_in.shape[-1] == N_IN and f_down.shape[1] == D_FF
    alpha = (2.0 * depth) ** 0.25
    blocks_per_seq = seq // TM
    t = bsz * seq

    weights = (w_in, a_out, b_out, c_out, w_o, f_up, f_down)
    w_in_b = w_in[0].astype(BF16)
    row3 = lambda p: p[:, None, :]
    a_ln_g3, a_ln_b3, c_conv_b3 = map(row3, (a_ln_g, a_ln_b, c_conv_b))
    ln1_g3, ln1_b3, ln2_g3, ln2_b3 = map(row3, (ln1_g, ln1_b, ln2_g, ln2_b))
    slabs = lambda p: p.reshape(depth, D_C // V7X_LANES, 1, V7X_LANES)
    c_ln_g4, c_ln_b4 = slabs(c_ln_g), slabs(c_ln_b)
    bs_wide = jnp.broadcast_to(a_bs[..., None], a_bs.shape + (HD,))

    xf = x.reshape(t, d)
    xb = xf.astype(BF16)
    for l in range(depth):
        if l == 0:
            act_a, (a_out_b, b_out_b, c_out_b, w_o_b) = _sgu_branch(
                xb, w_in_b, a_ln_g3, a_ln_b3, a_ws, bs_wide, l,
                cast_weights=(a_out, b_out, c_out, w_o), cast_layer=l)
            act_b, (f_down_b,) = _sconv_branch(xb, w_in_b, b_conv, l, blocks_per_seq,
                                               cast_weights=(f_down,), cast_layer=l)
            pre_c, (f_up_b,) = _cconv_branch(xb, w_in_b, c_conv, c_conv_b3, l, blocks_per_seq,
                                             cast_weights=(f_up,), cast_layer=l)
        else:
            w_in_b, a_out_b, b_out_b, c_out_b, w_o_b, f_up_b, f_down_b = weights_b
            act_a, _ = _sgu_branch(xb, w_in_b, a_ln_g3, a_ln_b3, a_ws, bs_wide, l)
            act_b, _ = _sconv_branch(xb, w_in_b, b_conv, l, blocks_per_seq)
            pre_c, _ = _cconv_branch(xb, w_in_b, c_conv, c_conv_b3, l, blocks_per_seq)
        act_c = _ln_silu(pre_c, c_ln_g4, c_ln_b4, l)
        m = _merge(xb, act_a, act_b, act_c, w_in_b, a_out_b, b_out_b, c_out_b, b_gate, l)
        mixed = _proj_residual(m, w_o_b, xf, alpha, TN_OUT, "out_proj")
        xf, xb = _ln_rows(mixed, ln1_g3, ln1_b3, l)
        next_weights = weights if l + 1 < depth else ()
        act, weights_b = _ffn_up(xb, f_up_b, f_conv, l, blocks_per_seq,
                                 cast_weights=next_weights, cast_layer=l + 1)
        ffn = _proj_residual(act, f_down_b, xf, alpha, TN_DOWN, "ffn_down")
        xf, xb = _ln_rows(ffn, ln2_g3, ln2_b3, l)
    return xf.reshape(bsz, seq, d)
```

```python
import functools
import math

import jax
import jax.numpy as jnp
from jax.experimental import pallas as pl
from jax.experimental.pallas import tpu as pltpu

F32 = jnp.float32
BF16 = jnp.bfloat16

D_MODEL = 4096
CHUNK = 128
HD = 128
D_A = D_MODEL // 2
H_A = D_A // HD
D_B = D_MODEL // 2
SC_K = 3
D_C = D_MODEL // 2
CF_K = 31
D_FF = 11008
FF_K = 3
N_BRANCH = 3
LN_EPS = 1e-5
OFF_A = 0
OFF_B = OFF_A + 2 * D_A
OFF_C = OFF_B + 3 * D_B
OFF_G = OFF_C + 2 * D_C
N_IN = OFF_G + N_BRANCH * D_MODEL

V7X_SUBLANES = 8
V7X_LANES = 128
BF16_ROWS = 2 * V7X_SUBLANES
V7X_VMEM_BYTES = 64 << 20

TM = 1024
ROWS = 256
FFN_PARTS = (128,) * 8
TN_FRONT = 256
TN_MERGE = 256
TN_OUT = 512
TF = 256
TN_DOWN = 256
TR_LN = 256
HALO_SHORT = V7X_SUBLANES
HALO_LONG = 32
ROW_STRIDE = 4
ROW_GROUP = ROW_STRIDE * V7X_SUBLANES
VMEM_LIMIT = V7X_VMEM_BYTES - (8 << 20)

assert ROWS % CHUNK == 0 and TM % ROWS == 0 and ROWS % ROW_GROUP == 0


def _sigmoid(v):
    return 0.5 * (1.0 + jnp.tanh(0.5 * v))


def _gelu_tanh(v):
    c = math.sqrt(2.0 / math.pi)
    return 0.5 * v * (1.0 + jnp.tanh(c * (v + 0.044715 * (v * v * v))))


def _dot(a, b):
    return jnp.dot(a, b, preferred_element_type=F32)


def _params(*sem):
    return pltpu.CompilerParams(dimension_semantics=sem, vmem_limit_bytes=VMEM_LIMIT)


def _col_tile(k, tn, tile_of):
    return pl.BlockSpec((k, tn), lambda *g: (0, tile_of(*g)))


def _cast_jobs(weights, layer, steps):
    in_specs, out_specs, out_shapes = [], [], []
    for w in weights:
        _, r, c = w.shape
        rows = next(m for m in range(BF16_ROWS, r + 1, BF16_ROWS) if r % m == 0 and r // m <= steps)
        last = r // rows - 1
        in_specs.append(lambda step_of, rows=rows, c=c, last=last: pl.BlockSpec(
            (None, rows, c), lambda *g: (layer, jnp.minimum(step_of(*g), last), 0)))
        out_specs.append(lambda step_of, rows=rows, c=c, last=last: pl.BlockSpec(
            (rows, c), lambda *g: (jnp.minimum(step_of(*g), last), 0)))
        out_shapes.append(jax.ShapeDtypeStruct((r, c), BF16))
    return in_specs, out_specs, out_shapes


def _split_refs(refs, n_in, n_out, n_cast):
    a, b, c = n_in + n_cast, n_in + n_cast + n_out, n_in + 2 * n_cast + n_out
    return refs[:n_in], refs[a:b], refs[c:], refs[n_in:a], refs[b:c]


def _cast_part(src_refs, dst_refs, part, nparts):
    for src, dst in zip(src_refs, dst_refs):
        c = src.shape[-1] // nparts
        dst[:, part * c:(part + 1) * c] = src[:, part * c:(part + 1) * c].astype(dst.dtype)


def _call_with_casts(body, args, in_specs, out_specs, out_shapes, *, grid, scratch_shapes,
                     semantics, name, cast_weights=(), cast_layer=0):
    n_in, n_out, n_cast = len(in_specs), len(out_specs), len(cast_weights)

    def step_of(*g):
        step = 0
        for index, extent in zip(g, grid):
            step = step * extent + index
        return step

    cast_in, cast_out, cast_shapes = _cast_jobs(cast_weights, cast_layer, math.prod(grid))

    def kernel(*refs):
        ins, outs, scratch, src, dst = _split_refs(refs, n_in, n_out, n_cast)
        body(*ins, *outs, *scratch, cast=functools.partial(_cast_part, src, dst))

    results = pl.pallas_call(
        kernel,
        grid=grid,
        in_specs=list(in_specs) + [spec(step_of) for spec in cast_in],
        out_specs=list(out_specs) + [spec(step_of) for spec in cast_out],
        out_shape=list(out_shapes) + cast_shapes,
        scratch_shapes=scratch_shapes,
        compiler_params=_params(*semantics),
        name=name,
    )(*args, *cast_weights)
    return results[:n_out], results[n_out:]


def _stage_block(x_ref, xs_ref, first_use):
    @pl.when(first_use)
    def _():
        xs_ref[...] = x_ref[...]


def _init_halo(stage_halo, carry, seq_start):
    @pl.when(seq_start)
    def _():
        stage_halo[...] = jnp.zeros(stage_halo.shape, F32)

    @pl.when(jnp.logical_not(seq_start))
    def _():
        stage_halo[...] = carry[...]


def _conv3(stage, cw, halo, start, rows):
    at = lambda shift: stage[halo + start - shift:halo + start - shift + rows, :]
    return cw[0:1, :] * at(2) + cw[1:2, :] * at(1) + cw[2:3, :] * at(0)


def _sgu_kernel(x_ref, wu_ref, wv_ref, g_ref, b_ref, ws_ref, bs_ref, o_ref, xs_ref, *, cast):
    tm, tn = o_ref.shape
    _stage_block(x_ref, xs_ref, pl.program_id(1) == 0)

    row = jax.lax.broadcasted_iota(jnp.int32, (CHUNK, CHUNK), 0)
    col = jax.lax.broadcasted_iota(jnp.int32, (CHUNK, CHUNK), 1)
    causal = row >= col
    heads = range(tn // HD)
    mix_w = [jnp.where(causal, ws_ref[h], 0.0).astype(BF16) for h in heads]

    def project(p):
        x = xs_ref[p * ROWS:(p + 1) * ROWS, :]
        return _dot(x, wu_ref[...]), _dot(x, wv_ref[...])

    def normalise(zv):
        v = _gelu_tanh(zv)
        out = []
        for h in heads:
            hs = slice(h * HD, (h + 1) * HD)
            vh = v[:, hs]
            mu = jnp.mean(vh, axis=-1, keepdims=True)
            dv = vh - mu
            var = jnp.mean(dv * dv, axis=-1, keepdims=True)
            out.append((dv * jax.lax.rsqrt(var + LN_EPS) * g_ref[:, hs] + b_ref[:, hs]).astype(BF16))
        return out

    def mix(p, zu, vn):
        u = _gelu_tanh(zu)
        for h in heads:
            hs = slice(h * HD, (h + 1) * HD)
            for c in range(ROWS // CHUNK):
                cs = slice(c * CHUNK, (c + 1) * CHUNK)
                mixed = _dot(mix_w[h], vn[h][cs, :]) + bs_ref[h]
                o_ref[p * ROWS + c * CHUNK:p * ROWS + (c + 1) * CHUNK, hs] = (
                    u[cs, hs] * mixed).astype(o_ref.dtype)

    nparts = tm // ROWS
    zu, zv = project(0)
    vn = normalise(zv)
    cast(0, nparts)
    for p in range(1, nparts):
        zu_next, zv_next = project(p)
        mix(p - 1, zu, vn)
        vn = normalise(zv_next)
        zu = zu_next
        cast(p, nparts)
    mix(nparts - 1, zu, vn)


def _sgu_branch(xb, w_in, ln_g, ln_b, ws, bs_wide, layer, **casts):
    t = xb.shape[0]
    tn = TN_FRONT
    (act,), cast_out = _call_with_casts(
        _sgu_kernel, (xb, w_in, w_in, ln_g, ln_b, ws, bs_wide),
        in_specs=[
            pl.BlockSpec((TM, D_MODEL), lambda i, j: (i, 0)),
            _col_tile(D_MODEL, tn, lambda i, j: OFF_A // tn + j),
            _col_tile(D_MODEL, tn, lambda i, j: (OFF_A + D_A) // tn + j),
            pl.BlockSpec((None, 1, tn), lambda i, j: (layer, 0, j)),
            pl.BlockSpec((None, 1, tn), lambda i, j: (layer, 0, j)),
            pl.BlockSpec((None, tn // HD, CHUNK, CHUNK), lambda i, j: (layer, j, 0, 0)),
            pl.BlockSpec((None, tn // HD, CHUNK, HD), lambda i, j: (layer, j, 0, 0)),
        ],
        out_specs=[pl.BlockSpec((TM, tn), lambda i, j: (i, j))],
        out_shapes=[jax.ShapeDtypeStruct((t, D_A), BF16)],
        grid=(t // TM, D_A // tn),
        scratch_shapes=[pltpu.VMEM((TM, D_MODEL), BF16)],
        semantics=("arbitrary", "arbitrary"),
        name="sgu_branch", **casts)
    return act, cast_out


def _sconv_kernel(x_ref, wb_ref, wc_ref, wh_ref, cw_ref, o_ref, xs_ref, p_ref, carry_ref,
                  *, blocks_per_seq, cast):
    i, j = pl.program_id(0), pl.program_id(1)
    tm = o_ref.shape[0]
    _stage_block(x_ref, xs_ref, j == 0)
    _init_halo(p_ref.at[0:HALO_SHORT], carry_ref.at[j], i % blocks_per_seq == 0)

    nparts = tm // ROWS
    for r in range(nparts):
        x = xs_ref[r * ROWS:(r + 1) * ROWS, :]
        p_ref[HALO_SHORT + r * ROWS:HALO_SHORT + (r + 1) * ROWS, :] = (
            _dot(x, wc_ref[...]) * _dot(x, wh_ref[...]))
        y = _conv3(p_ref, cw_ref[...], HALO_SHORT, r * ROWS, ROWS)
        o_ref[r * ROWS:(r + 1) * ROWS, :] = (_dot(x, wb_ref[...]) * y).astype(o_ref.dtype)
        cast(r, nparts)
    carry_ref[j] = p_ref[tm:tm + HALO_SHORT, :]


def _sconv_branch(xb, w_in, b_conv, layer, blocks_per_seq, **casts):
    t = xb.shape[0]
    tn = TN_FRONT
    nj = D_B // tn
    wspec = lambda off: _col_tile(D_MODEL, tn, lambda i, j: off // tn + j)
    (act,), cast_out = _call_with_casts(
        functools.partial(_sconv_kernel, blocks_per_seq=blocks_per_seq),
        (xb, w_in, w_in, w_in, b_conv),
        in_specs=[
            pl.BlockSpec((TM, D_MODEL), lambda i, j: (i, 0)),
            wspec(OFF_B), wspec(OFF_B + D_B), wspec(OFF_B + 2 * D_B),
            pl.BlockSpec((None, SC_K, tn), lambda i, j: (layer, 0, j)),
        ],
        out_specs=[pl.BlockSpec((TM, tn), lambda i, j: (i, j))],
        out_shapes=[jax.ShapeDtypeStruct((t, D_B), BF16)],
        grid=(t // TM, nj),
        scratch_shapes=[pltpu.VMEM((TM, D_MODEL), BF16),
                        pltpu.VMEM((HALO_SHORT + TM, tn), F32),
                        pltpu.VMEM((nj, HALO_SHORT, tn), F32)],
        semantics=("arbitrary", "arbitrary"),
        name="sconv_branch", **casts)
    return act, cast_out


def _cconv_kernel(x_ref, wa_ref, wg_ref, cw_ref, cb_ref, o_ref, xs_ref, ca_ref, carry_ref,
                  *, blocks_per_seq, cast):
    i, j = pl.program_id(0), pl.program_id(1)
    nslab, tm, _ = o_ref.shape
    _stage_block(x_ref, xs_ref, j == 0)
    _init_halo(ca_ref.at[:, 0:HALO_LONG], carry_ref.at[j], i % blocks_per_seq == 0)

    first = HALO_LONG - (CF_K - 1)

    def conv_group(s, q):
        ls = slice(s * V7X_LANES, (s + 1) * V7X_LANES)
        accs = [jnp.broadcast_to(cb_ref[:, ls], (V7X_SUBLANES, V7X_LANES))] * ROW_STRIDE
        for k in range(CF_K):
            wk = jnp.broadcast_to(cw_ref[k:k + 1, ls], (V7X_SUBLANES, V7X_LANES))
            for r in range(ROW_STRIDE):
                rows = pl.ds(q * ROW_GROUP + r + first + k, V7X_SUBLANES, stride=ROW_STRIDE)
                accs[r] = accs[r] + wk * ca_ref[s, rows, :]
        for r in range(ROW_STRIDE):
            o_ref[s, pl.ds(q * ROW_GROUP + r, V7X_SUBLANES, stride=ROW_STRIDE), :] = accs[r]

    groups_per_part = ROWS // ROW_GROUP

    nparts = tm // ROWS
    for part in range(nparts):
        x = xs_ref[part * ROWS:(part + 1) * ROWS, :]
        ca = _dot(x, wa_ref[...]) * _sigmoid(_dot(x, wg_ref[...]))
        data = slice(HALO_LONG + part * ROWS, HALO_LONG + (part + 1) * ROWS)
        for s in range(nslab):
            ca_ref[s, data, :] = ca[:, s * V7X_LANES:(s + 1) * V7X_LANES]
        cast(part, nparts)
        for s in range(nslab):
            for q in range(part * groups_per_part, (part + 1) * groups_per_part):
                conv_group(s, q)
    carry_ref[j] = ca_ref[:, tm:tm + HALO_LONG, :]


def _cconv_branch(xb, w_in, c_conv, c_conv_b, layer, blocks_per_seq, **casts):
    t = xb.shape[0]
    tn = TN_FRONT
    nj = D_C // tn
    nslab = tn // V7X_LANES
    wspec = lambda off: _col_tile(D_MODEL, tn, lambda i, j: off // tn + j)
    (pre,), cast_out = _call_with_casts(
        functools.partial(_cconv_kernel, blocks_per_seq=blocks_per_seq),
        (xb, w_in, w_in, c_conv, c_conv_b),
        in_specs=[
            pl.BlockSpec((TM, D_MODEL), lambda i, j: (i, 0)),
            wspec(OFF_C), wspec(OFF_C + D_C),
            pl.BlockSpec((None, CF_K, tn), lambda i, j: (layer, 0, j)),
            pl.BlockSpec((None, 1, tn), lambda i, j: (layer, 0, j)),
        ],
        out_specs=[pl.BlockSpec((nslab, TM, V7X_LANES), lambda i, j: (j, i, 0))],
        out_shapes=[jax.ShapeDtypeStruct((D_C // V7X_LANES, t, V7X_LANES), F32)],
        grid=(t // TM, nj),
        scratch_shapes=[pltpu.VMEM((TM, D_MODEL), BF16),
                        pltpu.VMEM((nslab, HALO_LONG + TM, V7X_LANES), F32),
                        pltpu.VMEM((nj, nslab, HALO_LONG, V7X_LANES), F32)],
        semantics=("arbitrary", "arbitrary"),
        name="cconv_branch", **casts)
    return pre, cast_out


def _ln_silu_kernel(v_ref, g_ref, b_ref, o_ref):
    nslab, _, lanes = v_ref.shape
    width = nslab * lanes
    v = v_ref[...]
    mu = jnp.sum(jnp.sum(v, axis=0), axis=-1, keepdims=True) / width
    dv = v - mu[None]
    var = jnp.sum(jnp.sum(dv * dv, axis=0), axis=-1, keepdims=True) / width
    y = dv * jax.lax.rsqrt(var + LN_EPS)[None] * g_ref[...] + b_ref[...]
    y = y * _sigmoid(y)
    for s in range(nslab):
        o_ref[:, s * lanes:(s + 1) * lanes] = y[s].astype(o_ref.dtype)


def _ln_silu(v, g, b, layer):
    nslab, t, lanes = v.shape
    par_spec = pl.BlockSpec((None, nslab, 1, lanes), lambda i: (layer, 0, 0, 0))
    return pl.pallas_call(
        _ln_silu_kernel,
        grid=(t // TR_LN,),
        in_specs=[pl.BlockSpec((nslab, TR_LN, lanes), lambda i: (0, i, 0)), par_spec, par_spec],
        out_specs=pl.BlockSpec((TR_LN, nslab * lanes), lambda i: (i, 0)),
        out_shape=jax.ShapeDtypeStruct((t, nslab * lanes), BF16),
        compiler_params=_params("parallel"),
        name="ln_silu",
    )(v, g, b)


def _ln_rows_kernel(v_ref, g_ref, b_ref, o_ref, ob_ref):
    v = v_ref[...]
    mu = jnp.mean(v, axis=-1, keepdims=True)
    dv = v - mu
    var = jnp.mean(dv * dv, axis=-1, keepdims=True)
    y = dv * jax.lax.rsqrt(var + LN_EPS) * g_ref[...] + b_ref[...]
    o_ref[...] = y
    ob_ref[...] = y.astype(ob_ref.dtype)


def _ln_rows(v, g, b, layer):
    t, c = v.shape
    row_spec = pl.BlockSpec((TR_LN, c), lambda i: (i, 0))
    par_spec = pl.BlockSpec((None, 1, c), lambda i: (layer, 0, 0))
    return pl.pallas_call(
        _ln_rows_kernel,
        grid=(t // TR_LN,),
        in_specs=[row_spec, par_spec, par_spec],
        out_specs=[row_spec, row_spec],
        out_shape=[jax.ShapeDtypeStruct((t, c), F32), jax.ShapeDtypeStruct((t, c), BF16)],
        compiler_params=_params("parallel"),
        name="ln_rows",
    )(v, g, b)


def _merge_kernel(x_ref, a_ref, b_ref, c_ref, wg0_ref, wg1_ref, wg2_ref, wa_ref, wb_ref, wc_ref,
                  bg_ref, o_ref):
    x = x_ref[...]
    g0 = _sigmoid(_dot(x, wg0_ref[...]) + bg_ref[0:1, :])
    m = g0 * _dot(a_ref[...], wa_ref[...])
    g1 = _sigmoid(_dot(x, wg1_ref[...]) + bg_ref[1:2, :])
    m = m + g1 * _dot(b_ref[...], wb_ref[...])
    g2 = _sigmoid(_dot(x, wg2_ref[...]) + bg_ref[2:3, :])
    m = m + g2 * _dot(c_ref[...], wc_ref[...])
    o_ref[...] = m.astype(o_ref.dtype)


def _merge(xb, act_a, act_b, act_c, w_in, a_out, b_out, c_out, b_gate, layer):
    t = xb.shape[0]
    tn = TN_MERGE
    resident = dict(pipeline_mode=pl.Buffered(1))
    gspec = lambda k: _col_tile(D_MODEL, tn, lambda i, j: (OFF_G + k * D_MODEL) // tn + j)
    ospec = lambda width: _col_tile(width, tn, lambda i, j: j)
    return pl.pallas_call(
        _merge_kernel,
        grid=(t // TM, D_MODEL // tn),
        in_specs=[
            pl.BlockSpec((TM, D_MODEL), lambda i, j: (i, 0), **resident),
            pl.BlockSpec((TM, D_A), lambda i, j: (i, 0), **resident),
            pl.BlockSpec((TM, D_B), lambda i, j: (i, 0), **resident),
            pl.BlockSpec((TM, D_C), lambda i, j: (i, 0), **resident),
            gspec(0), gspec(1), gspec(2),
            ospec(D_A), ospec(D_B), ospec(D_C),
            pl.BlockSpec((None, N_BRANCH, tn), lambda i, j: (layer, 0, j)),
        ],
        out_specs=pl.BlockSpec((TM, tn), lambda i, j: (i, j)),
        out_shape=jax.ShapeDtypeStruct((t, D_MODEL), BF16),
        compiler_params=_params("parallel", "arbitrary"),
        name="gate_merge",
    )(xb, act_a, act_b, act_c, w_in, w_in, w_in, a_out, b_out, c_out, b_gate)


def _proj_residual_kernel(a_ref, w_ref, x_ref, o_ref, *, alpha):
    half = a_ref.shape[0] // 2
    for rows in (slice(0, half), slice(half, 2 * half)):
        o_ref[rows, :] = alpha * x_ref[rows, :] + _dot(a_ref[rows, :], w_ref[...])


def _proj_residual(a, w, x, alpha, tn, name):
    t, k = a.shape
    return pl.pallas_call(
        functools.partial(_proj_residual_kernel, alpha=alpha),
        grid=(t // TM, D_MODEL // tn),
        in_specs=[pl.BlockSpec((TM, k), lambda i, j: (i, 0),
                               pipeline_mode=pl.Buffered(1) if k > D_MODEL else None),
                  _col_tile(k, tn, lambda i, j: j),
                  pl.BlockSpec((TM, tn), lambda i, j: (i, j))],
        out_specs=pl.BlockSpec((TM, tn), lambda i, j: (i, j)),
        out_shape=jax.ShapeDtypeStruct((t, D_MODEL), F32),
        compiler_params=_params("parallel", "arbitrary"),
        name=name,
    )(a, w, x)


def _ffn_up_kernel(x_ref, w1_ref, w2_ref, cw1_ref, cw2_ref, o_ref, xs_ref, h1_ref, h2_ref,
                   carry_ref, *, blocks_per_seq, cast):
    i, j = pl.program_id(0), pl.program_id(1)
    tm = o_ref.shape[0]
    _stage_block(x_ref, xs_ref, j == 0)
    seq_start = i % blocks_per_seq == 0
    _init_halo(h1_ref.at[0:HALO_SHORT], carry_ref.at[j, 0], seq_start)
    _init_halo(h2_ref.at[0:HALO_SHORT], carry_ref.at[j, 1], seq_start)

    assert sum(FFN_PARTS) == tm
    cast_parts = tm // ROWS
    start = 0
    for r, rows in enumerate(FFN_PARTS):
        x = xs_ref[start:start + rows, :]
        data = slice(HALO_SHORT + start, HALO_SHORT + start + rows)
        h1_ref[data, :] = _dot(x, w1_ref[...])
        h2_ref[data, :] = _dot(x, w2_ref[...])
        c1 = _conv3(h1_ref, cw1_ref[...], HALO_SHORT, start, rows)
        c2 = _conv3(h2_ref, cw2_ref[...], HALO_SHORT, start, rows)
        o_ref[start:start + rows, :] = (c1 * _sigmoid(c1) * c2).astype(o_ref.dtype)
        if r < cast_parts:
            cast(r, cast_parts)
        start += rows
    carry_ref[j, 0] = h1_ref[tm:tm + HALO_SHORT, :]
    carry_ref[j, 1] = h2_ref[tm:tm + HALO_SHORT, :]


def _ffn_up(xb, f_up, f_conv, layer, blocks_per_seq, **casts):
    t = xb.shape[0]
    nj = D_FF // TF
    (act,), cast_out = _call_with_casts(
        functools.partial(_ffn_up_kernel, blocks_per_seq=blocks_per_seq),
        (xb, f_up, f_up, f_conv, f_conv),
        in_specs=[
            pl.BlockSpec((TM, D_MODEL), lambda i, j: (i, 0)),
            _col_tile(D_MODEL, TF, lambda i, j: j),
            _col_tile(D_MODEL, TF, lambda i, j: nj + j),
            pl.BlockSpec((None, FF_K, TF), lambda i, j: (layer, 0, j)),
            pl.BlockSpec((None, FF_K, TF), lambda i, j: (layer, 0, nj + j)),
        ],
        out_specs=[pl.BlockSpec((TM, TF), lambda i, j: (i, j))],
        out_shapes=[jax.ShapeDtypeStruct((t, D_FF), BF16)],
        grid=(t // TM, nj),
        scratch_shapes=[pltpu.VMEM((TM, D_MODEL), BF16),
                        pltpu.VMEM((HALO_SHORT + TM, TF), F32),
                        pltpu.VMEM((HALO_SHORT + TM, TF), F32),
                        pltpu.VMEM((nj, 2, HALO_SHORT, TF), F32)],
        semantics=("arbitrary", "arbitrary"),
        name="ffn_up", **casts)
    return act, cast_out


def kernel(x, w_in, b_gate, a_ln_g, a_ln_b, a_ws, a_bs, a_out, b_conv, b_out, c_conv, c_conv_b,
           c_ln_g, c_ln_b, c_out, w_o, ln1_g, ln1_b, f_up, f_conv, f_down, ln2_g, ln2_b):
    bsz, seq, d = x.shape
    depth = w_in.shape[0]
    assert d == D_MODEL and seq % TM == 0 and w_in.shape[-1] == N_IN and f_down.shape[1] == D_FF
    alpha = (2.0 * depth) ** 0.25
    blocks_per_seq = seq // TM
    t = bsz * seq

    weights = (w_in, a_out, b_out, c_out, w_o, f_up, f_down)
    w_in_b = w_in[0].astype(BF16)
    row3 = lambda p: p[:, None, :]
    a_ln_g3, a_ln_b3, c_conv_b3 = map(row3, (a_ln_g, a_ln_b, c_conv_b))
    ln1_g3, ln1_b3, ln2_g3, ln2_b3 = map(row3, (ln1_g, ln1_b, ln2_g, ln2_b))
    slabs = lambda p: p.reshape(depth, D_C // V7X_LANES, 1, V7X_LANES)
    c_ln_g4, c_ln_b4 = slabs(c_ln_g), slabs(c_ln_b)
    bs_wide = jnp.broadcast_to(a_bs[..., None], a_bs.shape + (HD,))

    xf = x.reshape(t, d)
    xb = xf.astype(BF16)
    for l in range(depth):
        if l == 0:
            act_a, (a_out_b, b_out_b, c_out_b, w_o_b) = _sgu_branch(
                xb, w_in_b, a_ln_g3, a_ln_b3, a_ws, bs_wide, l,
                cast_weights=(a_out, b_out, c_out, w_o), cast_layer=l)
            act_b, (f_down_b,) = _sconv_branch(xb, w_in_b, b_conv, l, blocks_per_seq,
                                               cast_weights=(f_down,), cast_layer=l)
            pre_c, (f_up_b,) = _cconv_branch(xb, w_in_b, c_conv, c_conv_b3, l, blocks_per_seq,
                                             cast_weights=(f_up,), cast_layer=l)
        else:
            w_in_b, a_out_b, b_out_b, c_out_b, w_o_b, f_up_b, f_down_b = weights_b
            act_a, _ = _sgu_branch(xb, w_in_b, a_ln_g3, a_ln_b3, a_ws, bs_wide, l)
            act_b, _ = _sconv_branch(xb, w_in_b, b_conv, l, blocks_per_seq)
            pre_c, _ = _cconv_branch(xb, w_in_b, c_conv, c_conv_b3, l, blocks_per_seq)
        act_c = _ln_silu(pre_c, c_ln_g4, c_ln_b4, l)
        m = _merge(xb, act_a, act_b, act_c, w_in_b, a_out_b, b_out_b, c_out_b, b_gate, l)
        mixed = _proj_residual(m, w_o_b, xf, alpha, TN_OUT, "out_proj")
        xf, xb = _ln_rows(mixed, ln1_g3, ln1_b3, l)
        next_weights = weights if l + 1 < depth else ()
        act, weights_b = _ffn_up(xb, f_up_b, f_conv, l, blocks_per_seq,
                                 cast_weights=next_weights, cast_layer=l + 1)
        ffn = _proj_residual(act, f_down_b, xf, alpha, TN_DOWN, "ffn_down")
        xf, xb = _ln_rows(ffn, ln2_g3, ln2_b3, l)
    return xf.reshape(bsz, seq, d)
```

```python
import functools
import math

import jax
import jax.numpy as jnp
from jax.experimental import pallas as pl
from jax.experimental.pallas import tpu as pltpu

F32 = jnp.float32
BF16 = jnp.bfloat16

D_MODEL = 4096
CHUNK = 128
HD = 128
D_A = D_MODEL // 2
H_A = D_A // HD
D_B = D_MODEL // 2
SC_K = 3
D_C = D_MODEL // 2
CF_K = 31
D_FF = 11008
FF_K = 3
N_BRANCH = 3
LN_EPS = 1e-5
OFF_A = 0
OFF_B = OFF_A + 2 * D_A
OFF_C = OFF_B + 3 * D_B
OFF_G = OFF_C + 2 * D_C
N_IN = OFF_G + N_BRANCH * D_MODEL

V7X_SUBLANES = 8
V7X_LANES = 128
BF16_ROWS = 2 * V7X_SUBLANES
V7X_VMEM_BYTES = 64 << 20

TM = 1024
ROWS = 512
TN_FRONT = 256
TN_MERGE = 256
TN_OUT = 512
TF = 256
TN_DOWN = 256
TR_LN = 256
HALO_SHORT = V7X_SUBLANES
HALO_LONG = 32
ROW_STRIDE = 4
ROW_GROUP = ROW_STRIDE * V7X_SUBLANES
VMEM_LIMIT = V7X_VMEM_BYTES - (8 << 20)

assert ROWS % CHUNK == 0 and TM % ROWS == 0 and ROWS % ROW_GROUP == 0


def _sigmoid(v):
    return 0.5 * (1.0 + jnp.tanh(0.5 * v))


def _gelu_tanh(v):
    c = math.sqrt(2.0 / math.pi)
    return 0.5 * v * (1.0 + jnp.tanh(c * (v + 0.044715 * (v * v * v))))


def _dot(a, b):
    return jnp.dot(a, b, preferred_element_type=F32)


def _params(*sem):
    return pltpu.CompilerParams(dimension_semantics=sem, vmem_limit_bytes=VMEM_LIMIT)


def _col_tile(k, tn, tile_of):
    return pl.BlockSpec((k, tn), lambda *g: (0, tile_of(*g)))


def _cast_jobs(weights, layer, steps):
    in_specs, out_specs, out_shapes = [], [], []
    for w in weights:
        _, r, c = w.shape
        rows = next(m for m in range(BF16_ROWS, r + 1, BF16_ROWS) if r % m == 0 and r // m <= steps)
        last = r // rows - 1
        in_specs.append(lambda step_of, rows=rows, c=c, last=last: pl.BlockSpec(
            (None, rows, c), lambda *g: (layer, jnp.minimum(step_of(*g), last), 0)))
        out_specs.append(lambda step_of, rows=rows, c=c, last=last: pl.BlockSpec(
            (rows, c), lambda *g: (jnp.minimum(step_of(*g), last), 0)))
        out_shapes.append(jax.ShapeDtypeStruct((r, c), BF16))
    return in_specs, out_specs, out_shapes


def _split_refs(refs, n_in, n_out, n_cast):
    a, b, c = n_in + n_cast, n_in + n_cast + n_out, n_in + 2 * n_cast + n_out
    return refs[:n_in], refs[a:b], refs[c:], refs[n_in:a], refs[b:c]


def _cast_part(src_refs, dst_refs, part, nparts):
    for src, dst in zip(src_refs, dst_refs):
        c = src.shape[-1] // nparts
        dst[:, part * c:(part + 1) * c] = src[:, part * c:(part + 1) * c].astype(dst.dtype)


def _call_with_casts(body, args, in_specs, out_specs, out_shapes, *, grid, scratch_shapes,
                     semantics, name, cast_weights=(), cast_layer=0):
    n_in, n_out, n_cast = len(in_specs), len(out_specs), len(cast_weights)

    def step_of(*g):
        step = 0
        for index, extent in zip(g, grid):
            step = step * extent + index
        return step

    cast_in, cast_out, cast_shapes = _cast_jobs(cast_weights, cast_layer, math.prod(grid))

    def kernel(*refs):
        ins, outs, scratch, src, dst = _split_refs(refs, n_in, n_out, n_cast)
        body(*ins, *outs, *scratch, cast=functools.partial(_cast_part, src, dst))

    results = pl.pallas_call(
        kernel,
        grid=grid,
        in_specs=list(in_specs) + [spec(step_of) for spec in cast_in],
        out_specs=list(out_specs) + [spec(step_of) for spec in cast_out],
        out_shape=list(out_shapes) + cast_shapes,
        scratch_shapes=scratch_shapes,
        compiler_params=_params(*semantics),
        name=name,
    )(*args, *cast_weights)
    return results[:n_out], results[n_out:]


def _stage_block(x_ref, xs_ref, first_use):
    @pl.when(first_use)
    def _():
        xs_ref[...] = x_ref[...]


def _init_halo(stage_halo, carry, seq_start):
    @pl.when(seq_start)
    def _():
        stage_halo[...] = jnp.zeros(stage_halo.shape, F32)

    @pl.when(jnp.logical_not(seq_start))
    def _():
        stage_halo[...] = carry[...]


def _conv3(stage, cw, halo, start, rows):
    at = lambda shift: stage[halo + start - shift:halo + start - shift + rows, :]
    return cw[0:1, :] * at(2) + cw[1:2, :] * at(1) + cw[2:3, :] * at(0)


def _sgu_kernel(x_ref, wu_ref, wv_ref, g_ref, b_ref, ws_ref, bs_ref, o_ref, xs_ref, *, cast):
    tm, tn = o_ref.shape
    _stage_block(x_ref, xs_ref, pl.program_id(1) == 0)

    row = jax.lax.broadcasted_iota(jnp.int32, (CHUNK, CHUNK), 0)
    col = jax.lax.broadcasted_iota(jnp.int32, (CHUNK, CHUNK), 1)
    causal = row >= col
    heads = range(tn // HD)
    mix_w = [jnp.where(causal, ws_ref[h], 0.0).astype(BF16) for h in heads]

    def project(p):
        x = xs_ref[p * ROWS:(p + 1) * ROWS, :]
        return _dot(x, wu_ref[...]), _dot(x, wv_ref[...])

    def normalise(zv):
        v = _gelu_tanh(zv)
        out = []
        for h in heads:
            hs = slice(h * HD, (h + 1) * HD)
            vh = v[:, hs]
            mu = jnp.mean(vh, axis=-1, keepdims=True)
            dv = vh - mu
            var = jnp.mean(dv * dv, axis=-1, keepdims=True)
            out.append((dv * jax.lax.rsqrt(var + LN_EPS) * g_ref[:, hs] + b_ref[:, hs]).astype(BF16))
        return out

    def mix(p, zu, vn):
        u = _gelu_tanh(zu)
        for h in heads:
            hs = slice(h * HD, (h + 1) * HD)
            for c in range(ROWS // CHUNK):
                cs = slice(c * CHUNK, (c + 1) * CHUNK)
                mixed = _dot(mix_w[h], vn[h][cs, :]) + bs_ref[h]
                o_ref[p * ROWS + c * CHUNK:p * ROWS + (c + 1) * CHUNK, hs] = (
                    u[cs, hs] * mixed).astype(o_ref.dtype)

    nparts = tm // ROWS
    zu, zv = project(0)
    vn = normalise(zv)
    cast(0, nparts)
    for p in range(1, nparts):
        zu_next, zv_next = project(p)
        mix(p - 1, zu, vn)
        vn = normalise(zv_next)
        zu = zu_next
        cast(p, nparts)
    mix(nparts - 1, zu, vn)


def _sgu_branch(xb, w_in, ln_g, ln_b, ws, bs_wide, layer, **casts):
    t = xb.shape[0]
    tn = TN_FRONT
    (act,), cast_out = _call_with_casts(
        _sgu_kernel, (xb, w_in, w_in, ln_g, ln_b, ws, bs_wide),
        in_specs=[
            pl.BlockSpec((TM, D_MODEL), lambda i, j: (i, 0)),
            _col_tile(D_MODEL, tn, lambda i, j: OFF_A // tn + j),
            _col_tile(D_MODEL, tn, lambda i, j: (OFF_A + D_A) // tn + j),
            pl.BlockSpec((None, 1, tn), lambda i, j: (layer, 0, j)),
            pl.BlockSpec((None, 1, tn), lambda i, j: (layer, 0, j)),
            pl.BlockSpec((None, tn // HD, CHUNK, CHUNK), lambda i, j: (layer, j, 0, 0)),
            pl.BlockSpec((None, tn // HD, CHUNK, HD), lambda i, j: (layer, j, 0, 0)),
        ],
        out_specs=[pl.BlockSpec((TM, tn), lambda i, j: (i, j))],
        out_shapes=[jax.ShapeDtypeStruct((t, D_A), BF16)],
        grid=(t // TM, D_A // tn),
        scratch_shapes=[pltpu.VMEM((TM, D_MODEL), BF16)],
        semantics=("arbitrary", "arbitrary"),
        name="sgu_branch", **casts)
    return act, cast_out


def _sconv_kernel(x_ref, wb_ref, wc_ref, wh_ref, cw_ref, o_ref, xs_ref, p_ref, carry_ref,
                  *, blocks_per_seq, cast):
    i, j = pl.program_id(0), pl.program_id(1)
    tm = o_ref.shape[0]
    _stage_block(x_ref, xs_ref, j == 0)
    _init_halo(p_ref.at[0:HALO_SHORT], carry_ref.at[j], i % blocks_per_seq == 0)

    nparts = tm // ROWS
    for r in range(nparts):
        x = xs_ref[r * ROWS:(r + 1) * ROWS, :]
        p_ref[HALO_SHORT + r * ROWS:HALO_SHORT + (r + 1) * ROWS, :] = (
            _dot(x, wc_ref[...]) * _dot(x, wh_ref[...]))
        y = _conv3(p_ref, cw_ref[...], HALO_SHORT, r * ROWS, ROWS)
        o_ref[r * ROWS:(r + 1) * ROWS, :] = (_dot(x, wb_ref[...]) * y).astype(o_ref.dtype)
        cast(r, nparts)
    carry_ref[j] = p_ref[tm:tm + HALO_SHORT, :]


def _sconv_branch(xb, w_in, b_conv, layer, blocks_per_seq, **casts):
    t = xb.shape[0]
    tn = TN_FRONT
    nj = D_B // tn
    wspec = lambda off: _col_tile(D_MODEL, tn, lambda i, j: off // tn + j)
    (act,), cast_out = _call_with_casts(
        functools.partial(_sconv_kernel, blocks_per_seq=blocks_per_seq),
        (xb, w_in, w_in, w_in, b_conv),
        in_specs=[
            pl.BlockSpec((TM, D_MODEL), lambda i, j: (i, 0)),
            wspec(OFF_B), wspec(OFF_B + D_B), wspec(OFF_B + 2 * D_B),
            pl.BlockSpec((None, SC_K, tn), lambda i, j: (layer, 0, j)),
        ],
        out_specs=[pl.BlockSpec((TM, tn), lambda i, j: (i, j))],
        out_shapes=[jax.ShapeDtypeStruct((t, D_B), BF16)],
        grid=(t // TM, nj),
        scratch_shapes=[pltpu.VMEM((TM, D_MODEL), BF16),
                        pltpu.VMEM((HALO_SHORT + TM, tn), F32),
                        pltpu.VMEM((nj, HALO_SHORT, tn), F32)],
        semantics=("arbitrary", "arbitrary"),
        name="sconv_branch", **casts)
    return act, cast_out


def _cconv_kernel(x_ref, wa_ref, wg_ref, cw_ref, cb_ref, o_ref, xs_ref, ca_ref, carry_ref,
                  *, blocks_per_seq, cast):
    i, j = pl.program_id(0), pl.program_id(1)
    nslab, tm, _ = o_ref.shape
    _stage_block(x_ref, xs_ref, j == 0)
    _init_halo(ca_ref.at[:, 0:HALO_LONG], carry_ref.at[j], i % blocks_per_seq == 0)

    first = HALO_LONG - (CF_K - 1)

    def conv_group(s, q):
        ls = slice(s * V7X_LANES, (s + 1) * V7X_LANES)
        accs = [jnp.broadcast_to(cb_ref[:, ls], (V7X_SUBLANES, V7X_LANES))] * ROW_STRIDE
        for k in range(CF_K):
            wk = jnp.broadcast_to(cw_ref[k:k + 1, ls], (V7X_SUBLANES, V7X_LANES))
            for r in range(ROW_STRIDE):
                rows = pl.ds(q * ROW_GROUP + r + first + k, V7X_SUBLANES, stride=ROW_STRIDE)
                accs[r] = accs[r] + wk * ca_ref[s, rows, :]
        for r in range(ROW_STRIDE):
            o_ref[s, pl.ds(q * ROW_GROUP + r, V7X_SUBLANES, stride=ROW_STRIDE), :] = accs[r]

    groups_per_part = ROWS // ROW_GROUP

    nparts = tm // ROWS
    for part in range(nparts):
        x = xs_ref[part * ROWS:(part + 1) * ROWS, :]
        ca = _dot(x, wa_ref[...]) * _sigmoid(_dot(x, wg_ref[...]))
        data = slice(HALO_LONG + part * ROWS, HALO_LONG + (part + 1) * ROWS)
        for s in range(nslab):
            ca_ref[s, data, :] = ca[:, s * V7X_LANES:(s + 1) * V7X_LANES]
        cast(part, nparts)
        for s in range(nslab):
            for q in range(part * groups_per_part, (part + 1) * groups_per_part):
                conv_group(s, q)
    carry_ref[j] = ca_ref[:, tm:tm + HALO_LONG, :]


def _cconv_branch(xb, w_in, c_conv, c_conv_b, layer, blocks_per_seq, **casts):
    t = xb.shape[0]
    tn = TN_FRONT
    nj = D_C // tn
    nslab = tn // V7X_LANES
    wspec = lambda off: _col_tile(D_MODEL, tn, lambda i, j: off // tn + j)
    (pre,), cast_out = _call_with_casts(
        functools.partial(_cconv_kernel, blocks_per_seq=blocks_per_seq),
        (xb, w_in, w_in, c_conv, c_conv_b),
        in_specs=[
            pl.BlockSpec((TM, D_MODEL), lambda i, j: (i, 0)),
            wspec(OFF_C), wspec(OFF_C + D_C),
            pl.BlockSpec((None, CF_K, tn), lambda i, j: (layer, 0, j)),
            pl.BlockSpec((None, 1, tn), lambda i, j: (layer, 0, j)),
        ],
        out_specs=[pl.BlockSpec((nslab, TM, V7X_LANES), lambda i, j: (j, i, 0))],
        out_shapes=[jax.ShapeDtypeStruct((D_C // V7X_LANES, t, V7X_LANES), F32)],
        grid=(t // TM, nj),
        scratch_shapes=[pltpu.VMEM((TM, D_MODEL), BF16),
                        pltpu.VMEM((nslab, HALO_LONG + TM, V7X_LANES), F32),
                        pltpu.VMEM((nj, nslab, HALO_LONG, V7X_LANES), F32)],
        semantics=("arbitrary", "arbitrary"),
        name="cconv_branch", **casts)
    return pre, cast_out


def _ln_silu_kernel(v_ref, g_ref, b_ref, o_ref):
    nslab, _, lanes = v_ref.shape
    width = nslab * lanes
    v = v_ref[...]
    mu = jnp.sum(jnp.sum(v, axis=0), axis=-1, keepdims=True) / width
    dv = v - mu[None]
    var = jnp.sum(jnp.sum(dv * dv, axis=0), axis=-1, keepdims=True) / width
    y = dv * jax.lax.rsqrt(var + LN_EPS)[None] * g_ref[...] + b_ref[...]
    y = y * _sigmoid(y)
    for s in range(nslab):
        o_ref[:, s * lanes:(s + 1) * lanes] = y[s].astype(o_ref.dtype)


def _ln_silu(v, g, b, layer):
    nslab, t, lanes = v.shape
    par_spec = pl.BlockSpec((None, nslab, 1, lanes), lambda i: (layer, 0, 0, 0))
    return pl.pallas_call(
        _ln_silu_kernel,
        grid=(t // TR_LN,),
        in_specs=[pl.BlockSpec((nslab, TR_LN, lanes), lambda i: (0, i, 0)), par_spec, par_spec],
        out_specs=pl.BlockSpec((TR_LN, nslab * lanes), lambda i: (i, 0)),
        out_shape=jax.ShapeDtypeStruct((t, nslab * lanes), BF16),
        compiler_params=_params("parallel"),
        name="ln_silu",
    )(v, g, b)


def _ln_rows_kernel(v_ref, g_ref, b_ref, o_ref, ob_ref):
    v = v_ref[...]
    mu = jnp.mean(v, axis=-1, keepdims=True)
    dv = v - mu
    var = jnp.mean(dv * dv, axis=-1, keepdims=True)
    y = dv * jax.lax.rsqrt(var + LN_EPS) * g_ref[...] + b_ref[...]
    o_ref[...] = y
    ob_ref[...] = y.astype(ob_ref.dtype)


def _ln_rows(v, g, b, layer):
    t, c = v.shape
    row_spec = pl.BlockSpec((TR_LN, c), lambda i: (i, 0))
    par_spec = pl.BlockSpec((None, 1, c), lambda i: (layer, 0, 0))
    return pl.pallas_call(
        _ln_rows_kernel,
        grid=(t // TR_LN,),
        in_specs=[row_spec, par_spec, par_spec],
        out_specs=[row_spec, row_spec],
        out_shape=[jax.ShapeDtypeStruct((t, c), F32), jax.ShapeDtypeStruct((t, c), BF16)],
        compiler_params=_params("parallel"),
        name="ln_rows",
    )(v, g, b)


def _merge_kernel(x_ref, a_ref, b_ref, c_ref, wg0_ref, wg1_ref, wg2_ref, wa_ref, wb_ref, wc_ref,
                  bg_ref, o_ref):
    x = x_ref[...]
    g0 = _sigmoid(_dot(x, wg0_ref[...]) + bg_ref[0:1, :])
    m = g0 * _dot(a_ref[...], wa_ref[...])
    g1 = _sigmoid(_dot(x, wg1_ref[...]) + bg_ref[1:2, :])
    m = m + g1 * _dot(b_ref[...], wb_ref[...])
    g2 = _sigmoid(_dot(x, wg2_ref[...]) + bg_ref[2:3, :])
    m = m + g2 * _dot(c_ref[...], wc_ref[...])
    o_ref[...] = m.astype(o_ref.dtype)


def _merge(xb, act_a, act_b, act_c, w_in, a_out, b_out, c_out, b_gate, layer):
    t = xb.shape[0]
    tn = TN_MERGE
    resident = dict(pipeline_mode=pl.Buffered(1))
    gspec = lambda k: _col_tile(D_MODEL, tn, lambda i, j: (OFF_G + k * D_MODEL) // tn + j)
    ospec = lambda width: _col_tile(width, tn, lambda i, j: j)
    return pl.pallas_call(
        _merge_kernel,
        grid=(t // TM, D_MODEL // tn),
        in_specs=[
            pl.BlockSpec((TM, D_MODEL), lambda i, j: (i, 0), **resident),
            pl.BlockSpec((TM, D_A), lambda i, j: (i, 0), **resident),
            pl.BlockSpec((TM, D_B), lambda i, j: (i, 0), **resident),
            pl.BlockSpec((TM, D_C), lambda i, j: (i, 0), **resident),
            gspec(0), gspec(1), gspec(2),
            ospec(D_A), ospec(D_B), ospec(D_C),
            pl.BlockSpec((None, N_BRANCH, tn), lambda i, j: (layer, 0, j)),
        ],
        out_specs=pl.BlockSpec((TM, tn), lambda i, j: (i, j)),
        out_shape=jax.ShapeDtypeStruct((t, D_MODEL), BF16),
        compiler_params=_params("parallel", "arbitrary"),
        name="gate_merge",
    )(xb, act_a, act_b, act_c, w_in, w_in, w_in, a_out, b_out, c_out, b_gate)


def _proj_residual_kernel(a_ref, w_ref, x_ref, o_ref, *, alpha):
    half = a_ref.shape[0] // 2
    for rows in (slice(0, half), slice(half, 2 * half)):
        o_ref[rows, :] = alpha * x_ref[rows, :] + _dot(a_ref[rows, :], w_ref[...])


def _proj_residual(a, w, x, alpha, tn, name):
    t, k = a.shape
    return pl.pallas_call(
        functools.partial(_proj_residual_kernel, alpha=alpha),
        grid=(t // TM, D_MODEL // tn),
        in_specs=[pl.BlockSpec((TM, k), lambda i, j: (i, 0),
                               pipeline_mode=pl.Buffered(1) if k > D_MODEL else None),
                  _col_tile(k, tn, lambda i, j: j),
                  pl.BlockSpec((TM, tn), lambda i, j: (i, j))],
        out_specs=pl.BlockSpec((TM, tn), lambda i, j: (i, j)),
        out_shape=jax.ShapeDtypeStruct((t, D_MODEL), F32),
        compiler_params=_params("parallel", "arbitrary"),
        name=name,
    )(a, w, x)


def _ffn_up_kernel(x_ref, w1_ref, w2_ref, cw1_ref, cw2_ref, o_ref, xs_ref, h1_ref, h2_ref,
                   carry_ref, *, blocks_per_seq, cast):
    i, j = pl.program_id(0), pl.program_id(1)
    tm = o_ref.shape[0]
    _stage_block(x_ref, xs_ref, j == 0)
    seq_start = i % blocks_per_seq == 0
    _init_halo(h1_ref.at[0:HALO_SHORT], carry_ref.at[j, 0], seq_start)
    _init_halo(h2_ref.at[0:HALO_SHORT], carry_ref.at[j, 1], seq_start)

    nparts = tm // ROWS
    for r in range(nparts):
        x = xs_ref[r * ROWS:(r + 1) * ROWS, :]
        data = slice(HALO_SHORT + r * ROWS, HALO_SHORT + (r + 1) * ROWS)
        h1_ref[data, :] = _dot(x, w1_ref[...])
        h2_ref[data, :] = _dot(x, w2_ref[...])
        c1 = _conv3(h1_ref, cw1_ref[...], HALO_SHORT, r * ROWS, ROWS)
        c2 = _conv3(h2_ref, cw2_ref[...], HALO_SHORT, r * ROWS, ROWS)
        o_ref[r * ROWS:(r + 1) * ROWS, :] = (c1 * _sigmoid(c1) * c2).astype(o_ref.dtype)
        cast(r, nparts)
    carry_ref[j, 0] = h1_ref[tm:tm + HALO_SHORT, :]
    carry_ref[j, 1] = h2_ref[tm:tm + HALO_SHORT, :]


def _ffn_up(xb, f_up, f_conv, layer, blocks_per_seq, **casts):
    t = xb.shape[0]
    nj = D_FF // TF
    (act,), cast_out = _call_with_casts(
        functools.partial(_ffn_up_kernel, blocks_per_seq=blocks_per_seq),
        (xb, f_up, f_up, f_conv, f_conv),
        in_specs=[
            pl.BlockSpec((TM, D_MODEL), lambda i, j: (i, 0)),
            _col_tile(D_MODEL, TF, lambda i, j: j),
            _col_tile(D_MODEL, TF, lambda i, j: nj + j),
            pl.BlockSpec((None, FF_K, TF), lambda i, j: (layer, 0, j)),
            pl.BlockSpec((None, FF_K, TF), lambda i, j: (layer, 0, nj + j)),
        ],
        out_specs=[pl.BlockSpec((TM, TF), lambda i, j: (i, j))],
        out_shapes=[jax.ShapeDtypeStruct((t, D_FF), BF16)],
        grid=(t // TM, nj),
        scratch_shapes=[pltpu.VMEM((TM, D_MODEL), BF16),
                        pltpu.VMEM((HALO_SHORT + TM, TF), F32),
                        pltpu.VMEM((HALO_SHORT + TM, TF), F32),
                        pltpu.VMEM((nj, 2, HALO_SHORT, TF), F32)],
        semantics=("arbitrary", "arbitrary"),
        name="ffn_up", **casts)
    return act, cast_out


def kernel(x, w_in, b_gate, a_ln_g, a_ln_b, a_ws, a_bs, a_out, b_conv, b_out, c_conv, c_conv_b,
           c_ln_g, c_ln_b, c_out, w_o, ln1_g, ln1_b, f_up, f_conv, f_down, ln2_g, ln2_b):
    bsz, seq, d = x.shape
    depth = w_in.shape[0]
    assert d == D_MODEL and seq % TM == 0 and w_in.shape[-1] == N_IN and f_down.shape[1] == D_FF
    alpha = (2.0 * depth) ** 0.25
    blocks_per_seq = seq // TM
    t = bsz * seq

    weights = (w_in, a_out, b_out, c_out, w_o, f_up, f_down)
    w_in_b = w_in[0].astype(BF16)
    row3 = lambda p: p[:, None, :]
    a_ln_g3, a_ln_b3, c_conv_b3 = map(row3, (a_ln_g, a_ln_b, c_conv_b))
    ln1_g3, ln1_b3, ln2_g3, ln2_b3 = map(row3, (ln1_g, ln1_b, ln2_g, ln2_b))
    slabs = lambda p: p.reshape(depth, D_C // V7X_LANES, 1, V7X_LANES)
    c_ln_g4, c_ln_b4 = slabs(c_ln_g), slabs(c_ln_b)
    bs_wide = jnp.broadcast_to(a_bs[..., None], a_bs.shape + (HD,))

    xf = x.reshape(t, d)
    xb = xf.astype(BF16)
    for l in range(depth):
        if l == 0:
            act_a, (a_out_b, b_out_b, c_out_b, w_o_b) = _sgu_branch(
                xb, w_in_b, a_ln_g3, a_ln_b3, a_ws, bs_wide, l,
                cast_weights=(a_out, b_out, c_out, w_o), cast_layer=l)
            act_b, (f_down_b,) = _sconv_branch(xb, w_in_b, b_conv, l, blocks_per_seq,
                                               cast_weights=(f_down,), cast_layer=l)
            pre_c, (f_up_b,) = _cconv_branch(xb, w_in_b, c_conv, c_conv_b3, l, blocks_per_seq,
                                             cast_weights=(f_up,), cast_layer=l)
        else:
            w_in_b, a_out_b, b_out_b, c_out_b, w_o_b, f_up_b, f_down_b = weights_b
            act_a, _ = _sgu_branch(xb, w_in_b, a_ln_g3, a_ln_b3, a_ws, bs_wide, l)
            act_b, _ = _sconv_branch(xb, w_in_b, b_conv, l, blocks_per_seq)
            pre_c, _ = _cconv_branch(xb, w_in_b, c_conv, c_conv_b3, l, blocks_per_seq)
        act_c = _ln_silu(pre_c, c_ln_g4, c_ln_b4, l)
        m = _merge(xb, act_a, act_b, act_c, w_in_b, a_out_b, b_out_b, c_out_b, b_gate, l)
        mixed = _proj_residual(m, w_o_b, xf, alpha, TN_OUT, "out_proj")
        xf, xb = _ln_rows(mixed, ln1_g3, ln1_b3, l)
        next_weights = weights if l + 1 < depth else ()
        act, weights_b = _ffn_up(xb, f_up_b, f_conv, l, blocks_per_seq,
                                 cast_weights=next_weights, cast_layer=l + 1)
        ffn = _proj_residual(act, f_down_b, xf, alpha, TN_DOWN, "ffn_down")
        xf, xb = _ln_rows(ffn, ln2_g3, ln2_b3, l)
    return xf.reshape(bsz, seq, d)
```

```python
import functools
import math

import jax
import jax.numpy as jnp
from jax.experimental import pallas as pl
from jax.experimental.pallas import tpu as pltpu

F32 = jnp.float32
BF16 = jnp.bfloat16

D_MODEL = 4096
CHUNK = 128
HD = 128
D_A = D_MODEL // 2
H_A = D_A // HD
D_B = D_MODEL // 2
SC_K = 3
D_C = D_MODEL // 2
CF_K = 31
D_FF = 11008
FF_K = 3
N_BRANCH = 3
LN_EPS = 1e-5
OFF_A = 0
OFF_B = OFF_A + 2 * D_A
OFF_C = OFF_B + 3 * D_B
OFF_G = OFF_C + 2 * D_C
N_IN = OFF_G + N_BRANCH * D_MODEL

V7X_SUBLANES = 8
V7X_LANES = 128
BF16_ROWS = 2 * V7X_SUBLANES
V7X_VMEM_BYTES = 64 << 20

TM = 1024
ROWS = 512
FFN_PARTS = (512, 512)
TN_FRONT = 256
TN_MERGE = 256
TN_OUT = 512
TF = 256
TN_DOWN = 256
TR_LN = 256
HALO_SHORT = V7X_SUBLANES
HALO_LONG = 32
ROW_STRIDE = 4
ROW_GROUP = ROW_STRIDE * V7X_SUBLANES
VMEM_LIMIT = V7X_VMEM_BYTES - (8 << 20)

assert ROWS % CHUNK == 0 and TM % ROWS == 0 and ROWS % ROW_GROUP == 0


def _sigmoid(v):
    return 0.5 * (1.0 + jnp.tanh(0.5 * v))


def _gelu_tanh(v):
    c = math.sqrt(2.0 / math.pi)
    return 0.5 * v * (1.0 + jnp.tanh(c * (v + 0.044715 * (v * v * v))))


def _dot(a, b):
    return jnp.dot(a, b, preferred_element_type=F32)


def _params(*sem):
    return pltpu.CompilerParams(dimension_semantics=sem, vmem_limit_bytes=VMEM_LIMIT)


def _col_tile(k, tn, tile_of):
    return pl.BlockSpec((k, tn), lambda *g: (0, tile_of(*g)))


def _cast_jobs(weights, layer, steps):
    in_specs, out_specs, out_shapes = [], [], []
    for w in weights:
        _, r, c = w.shape
        rows = next(m for m in range(BF16_ROWS, r + 1, BF16_ROWS) if r % m == 0 and r // m <= steps)
        last = r // rows - 1
        in_specs.append(lambda step_of, rows=rows, c=c, last=last: pl.BlockSpec(
            (None, rows, c), lambda *g: (layer, jnp.minimum(step_of(*g), last), 0)))
        out_specs.append(lambda step_of, rows=rows, c=c, last=last: pl.BlockSpec(
            (rows, c), lambda *g: (jnp.minimum(step_of(*g), last), 0)))
        out_shapes.append(jax.ShapeDtypeStruct((r, c), BF16))
    return in_specs, out_specs, out_shapes


def _split_refs(refs, n_in, n_out, n_cast):
    a, b, c = n_in + n_cast, n_in + n_cast + n_out, n_in + 2 * n_cast + n_out
    return refs[:n_in], refs[a:b], refs[c:], refs[n_in:a], refs[b:c]


def _cast_part(src_refs, dst_refs, part, nparts):
    for src, dst in zip(src_refs, dst_refs):
        c = src.shape[-1] // nparts
        dst[:, part * c:(part + 1) * c] = src[:, part * c:(part + 1) * c].astype(dst.dtype)


def _call_with_casts(body, args, in_specs, out_specs, out_shapes, *, grid, scratch_shapes,
                     semantics, name, cast_weights=(), cast_layer=0):
    n_in, n_out, n_cast = len(in_specs), len(out_specs), len(cast_weights)

    def step_of(*g):
        step = 0
        for index, extent in zip(g, grid):
            step = step * extent + index
        return step

    cast_in, cast_out, cast_shapes = _cast_jobs(cast_weights, cast_layer, math.prod(grid))

    def kernel(*refs):
        ins, outs, scratch, src, dst = _split_refs(refs, n_in, n_out, n_cast)
        body(*ins, *outs, *scratch, cast=functools.partial(_cast_part, src, dst))

    results = pl.pallas_call(
        kernel,
        grid=grid,
        in_specs=list(in_specs) + [spec(step_of) for spec in cast_in],
        out_specs=list(out_specs) + [spec(step_of) for spec in cast_out],
        out_shape=list(out_shapes) + cast_shapes,
        scratch_shapes=scratch_shapes,
        compiler_params=_params(*semantics),
        name=name,
    )(*args, *cast_weights)
    return results[:n_out], results[n_out:]


def _stage_block(x_ref, xs_ref, first_use):
    @pl.when(first_use)
    def _():
        xs_ref[...] = x_ref[...]


def _init_halo(stage_halo, carry, seq_start):
    @pl.when(seq_start)
    def _():
        stage_halo[...] = jnp.zeros(stage_halo.shape, F32)

    @pl.when(jnp.logical_not(seq_start))
    def _():
        stage_halo[...] = carry[...]


def _conv3(stage, cw, halo, start, rows):
    at = lambda shift: stage[halo + start - shift:halo + start - shift + rows, :]
    return cw[0:1, :] * at(2) + cw[1:2, :] * at(1) + cw[2:3, :] * at(0)


def _sgu_kernel(x_ref, wu_ref, wv_ref, g_ref, b_ref, ws_ref, bs_ref, o_ref, xs_ref, *, cast):
    tm, tn = o_ref.shape
    _stage_block(x_ref, xs_ref, pl.program_id(1) == 0)

    row = jax.lax.broadcasted_iota(jnp.int32, (CHUNK, CHUNK), 0)
    col = jax.lax.broadcasted_iota(jnp.int32, (CHUNK, CHUNK), 1)
    causal = row >= col
    heads = range(tn // HD)
    mix_w = [jnp.where(causal, ws_ref[h], 0.0).astype(BF16) for h in heads]

    def project(p):
        x = xs_ref[p * ROWS:(p + 1) * ROWS, :]
        return _dot(x, wu_ref[...]), _dot(x, wv_ref[...])

    def normalise(zv):
        v = _gelu_tanh(zv)
        out = []
        for h in heads:
            hs = slice(h * HD, (h + 1) * HD)
            vh = v[:, hs]
            mu = jnp.mean(vh, axis=-1, keepdims=True)
            dv = vh - mu
            var = jnp.mean(dv * dv, axis=-1, keepdims=True)
            out.append((dv * jax.lax.rsqrt(var + LN_EPS) * g_ref[:, hs] + b_ref[:, hs]).astype(BF16))
        return out

    def mix(p, zu, vn):
        u = _gelu_tanh(zu)
        for h in heads:
            hs = slice(h * HD, (h + 1) * HD)
            for c in range(ROWS // CHUNK):
                cs = slice(c * CHUNK, (c + 1) * CHUNK)
                mixed = _dot(mix_w[h], vn[h][cs, :]) + bs_ref[h]
                o_ref[p * ROWS + c * CHUNK:p * ROWS + (c + 1) * CHUNK, hs] = (
                    u[cs, hs] * mixed).astype(o_ref.dtype)

    nparts = tm // ROWS
    zu, zv = project(0)
    vn = normalise(zv)
    cast(0, nparts)
    for p in range(1, nparts):
        zu_next, zv_next = project(p)
        mix(p - 1, zu, vn)
        vn = normalise(zv_next)
        zu = zu_next
        cast(p, nparts)
    mix(nparts - 1, zu, vn)


def _sgu_branch(xb, w_in, ln_g, ln_b, ws, bs_wide, layer, **casts):
    t = xb.shape[0]
    tn = TN_FRONT
    (act,), cast_out = _call_with_casts(
        _sgu_kernel, (xb, w_in, w_in, ln_g, ln_b, ws, bs_wide),
        in_specs=[
            pl.BlockSpec((TM, D_MODEL), lambda i, j: (i, 0)),
            _col_tile(D_MODEL, tn, lambda i, j: OFF_A // tn + j),
            _col_tile(D_MODEL, tn, lambda i, j: (OFF_A + D_A) // tn + j),
            pl.BlockSpec((None, 1, tn), lambda i, j: (layer, 0, j)),
            pl.BlockSpec((None, 1, tn), lambda i, j: (layer, 0, j)),
            pl.BlockSpec((None, tn // HD, CHUNK, CHUNK), lambda i, j: (layer, j, 0, 0)),
            pl.BlockSpec((None, tn // HD, CHUNK, HD), lambda i, j: (layer, j, 0, 0)),
        ],
        out_specs=[pl.BlockSpec((TM, tn), lambda i, j: (i, j))],
        out_shapes=[jax.ShapeDtypeStruct((t, D_A), BF16)],
        grid=(t // TM, D_A // tn),
        scratch_shapes=[pltpu.VMEM((TM, D_MODEL), BF16)],
        semantics=("arbitrary", "arbitrary"),
        name="sgu_branch", **casts)
    return act, cast_out


def _sconv_kernel(x_ref, wb_ref, wc_ref, wh_ref, cw_ref, o_ref, xs_ref, p_ref, carry_ref,
                  *, blocks_per_seq, cast):
    i, j = pl.program_id(0), pl.program_id(1)
    tm = o_ref.shape[0]
    _stage_block(x_ref, xs_ref, j == 0)
    _init_halo(p_ref.at[0:HALO_SHORT], carry_ref.at[j], i % blocks_per_seq == 0)

    nparts = tm // ROWS
    for r in range(nparts):
        x = xs_ref[r * ROWS:(r + 1) * ROWS, :]
        p_ref[HALO_SHORT + r * ROWS:HALO_SHORT + (r + 1) * ROWS, :] = (
            _dot(x, wc_ref[...]) * _dot(x, wh_ref[...]))
        y = _conv3(p_ref, cw_ref[...], HALO_SHORT, r * ROWS, ROWS)
        o_ref[r * ROWS:(r + 1) * ROWS, :] = (_dot(x, wb_ref[...]) * y).astype(o_ref.dtype)
        cast(r, nparts)
    carry_ref[j] = p_ref[tm:tm + HALO_SHORT, :]


def _sconv_branch(xb, w_in, b_conv, layer, blocks_per_seq, **casts):
    t = xb.shape[0]
    tn = TN_FRONT
    nj = D_B // tn
    wspec = lambda off: _col_tile(D_MODEL, tn, lambda i, j: off // tn + j)
    (act,), cast_out = _call_with_casts(
        functools.partial(_sconv_kernel, blocks_per_seq=blocks_per_seq),
        (xb, w_in, w_in, w_in, b_conv),
        in_specs=[
            pl.BlockSpec((TM, D_MODEL), lambda i, j: (i, 0)),
            wspec(OFF_B), wspec(OFF_B + D_B), wspec(OFF_B + 2 * D_B),
            pl.BlockSpec((None, SC_K, tn), lambda i, j: (layer, 0, j)),
        ],
        out_specs=[pl.BlockSpec((TM, tn), lambda i, j: (i, j))],
        out_shapes=[jax.ShapeDtypeStruct((t, D_B), BF16)],
        grid=(t // TM, nj),
        scratch_shapes=[pltpu.VMEM((TM, D_MODEL), BF16),
                        pltpu.VMEM((HALO_SHORT + TM, tn), F32),
                        pltpu.VMEM((nj, HALO_SHORT, tn), F32)],
        semantics=("arbitrary", "arbitrary"),
        name="sconv_branch", **casts)
    return act, cast_out


def _cconv_kernel(x_ref, wa_ref, wg_ref, cw_ref, cb_ref, o_ref, xs_ref, ca_ref, carry_ref,
                  *, blocks_per_seq, cast):
    i, j = pl.program_id(0), pl.program_id(1)
    nslab, tm, _ = o_ref.shape
    _stage_block(x_ref, xs_ref, j == 0)
    _init_halo(ca_ref.at[:, 0:HALO_LONG], carry_ref.at[j], i % blocks_per_seq == 0)

    first = HALO_LONG - (CF_K - 1)

    def conv_group(s, q):
        ls = slice(s * V7X_LANES, (s + 1) * V7X_LANES)
        accs = [jnp.broadcast_to(cb_ref[:, ls], (V7X_SUBLANES, V7X_LANES))] * ROW_STRIDE
        for k in range(CF_K):
            wk = jnp.broadcast_to(cw_ref[k:k + 1, ls], (V7X_SUBLANES, V7X_LANES))
            for r in range(ROW_STRIDE):
                rows = pl.ds(q * ROW_GROUP + r + first + k, V7X_SUBLANES, stride=ROW_STRIDE)
                accs[r] = accs[r] + wk * ca_ref[s, rows, :]
        for r in range(ROW_STRIDE):
            o_ref[s, pl.ds(q * ROW_GROUP + r, V7X_SUBLANES, stride=ROW_STRIDE), :] = accs[r]

    groups_per_part = ROWS // ROW_GROUP

    nparts = tm // ROWS
    for part in range(nparts):
        x = xs_ref[part * ROWS:(part + 1) * ROWS, :]
        ca = _dot(x, wa_ref[...]) * _sigmoid(_dot(x, wg_ref[...]))
        data = slice(HALO_LONG + part * ROWS, HALO_LONG + (part + 1) * ROWS)
        for s in range(nslab):
            ca_ref[s, data, :] = ca[:, s * V7X_LANES:(s + 1) * V7X_LANES]
        cast(part, nparts)
        for s in range(nslab):
            for q in range(part * groups_per_part, (part + 1) * groups_per_part):
                conv_group(s, q)
    carry_ref[j] = ca_ref[:, tm:tm + HALO_LONG, :]


def _cconv_branch(xb, w_in, c_conv, c_conv_b, layer, blocks_per_seq, **casts):
    t = xb.shape[0]
    tn = TN_FRONT
    nj = D_C // tn
    nslab = tn // V7X_LANES
    wspec = lambda off: _col_tile(D_MODEL, tn, lambda i, j: off // tn + j)
    (pre,), cast_out = _call_with_casts(
        functools.partial(_cconv_kernel, blocks_per_seq=blocks_per_seq),
        (xb, w_in, w_in, c_conv, c_conv_b),
        in_specs=[
            pl.BlockSpec((TM, D_MODEL), lambda i, j: (i, 0)),
            wspec(OFF_C), wspec(OFF_C + D_C),
            pl.BlockSpec((None, CF_K, tn), lambda i, j: (layer, 0, j)),
            pl.BlockSpec((None, 1, tn), lambda i, j: (layer, 0, j)),
        ],
        out_specs=[pl.BlockSpec((nslab, TM, V7X_LANES), lambda i, j: (j, i, 0))],
        out_shapes=[jax.ShapeDtypeStruct((D_C // V7X_LANES, t, V7X_LANES), F32)],
        grid=(t // TM, nj),
        scratch_shapes=[pltpu.VMEM((TM, D_MODEL), BF16),
                        pltpu.VMEM((nslab, HALO_LONG + TM, V7X_LANES), F32),
                        pltpu.VMEM((nj, nslab, HALO_LONG, V7X_LANES), F32)],
        semantics=("arbitrary", "arbitrary"),
        name="cconv_branch", **casts)
    return pre, cast_out


def _ln_silu_kernel(v_ref, g_ref, b_ref, o_ref):
    nslab, _, lanes = v_ref.shape
    width = nslab * lanes
    v = v_ref[...]
    mu = jnp.sum(jnp.sum(v, axis=0), axis=-1, keepdims=True) / width
    dv = v - mu[None]
    var = jnp.sum(jnp.sum(dv * dv, axis=0), axis=-1, keepdims=True) / width
    y = dv * jax.lax.rsqrt(var + LN_EPS)[None] * g_ref[...] + b_ref[...]
    y = y * _sigmoid(y)
    for s in range(nslab):
        o_ref[:, s * lanes:(s + 1) * lanes] = y[s].astype(o_ref.dtype)


def _ln_silu(v, g, b, layer):
    nslab, t, lanes = v.shape
    par_spec = pl.BlockSpec((None, nslab, 1, lanes), lambda i: (layer, 0, 0, 0))
    return pl.pallas_call(
        _ln_silu_kernel,
        grid=(t // TR_LN,),
        in_specs=[pl.BlockSpec((nslab, TR_LN, lanes), lambda i: (0, i, 0)), par_spec, par_spec],
        out_specs=pl.BlockSpec((TR_LN, nslab * lanes), lambda i: (i, 0)),
        out_shape=jax.ShapeDtypeStruct((t, nslab * lanes), BF16),
        compiler_params=_params("parallel"),
        name="ln_silu",
    )(v, g, b)


def _ln_rows_kernel(v_ref, g_ref, b_ref, o_ref, ob_ref):
    v = v_ref[...]
    mu = jnp.mean(v, axis=-1, keepdims=True)
    dv = v - mu
    var = jnp.mean(dv * dv, axis=-1, keepdims=True)
    y = dv * jax.lax.rsqrt(var + LN_EPS) * g_ref[...] + b_ref[...]
    o_ref[...] = y
    ob_ref[...] = y.astype(ob_ref.dtype)


def _ln_rows(v, g, b, layer):
    t, c = v.shape
    row_spec = pl.BlockSpec((TR_LN, c), lambda i: (i, 0))
    par_spec = pl.BlockSpec((None, 1, c), lambda i: (layer, 0, 0))
    return pl.pallas_call(
        _ln_rows_kernel,
        grid=(t // TR_LN,),
        in_specs=[row_spec, par_spec, par_spec],
        out_specs=[row_spec, row_spec],
        out_shape=[jax.ShapeDtypeStruct((t, c), F32), jax.ShapeDtypeStruct((t, c), BF16)],
        compiler_params=_params("parallel"),
        name="ln_rows",
    )(v, g, b)


def _merge_kernel(x_ref, a_ref, b_ref, c_ref, wg0_ref, wg1_ref, wg2_ref, wa_ref, wb_ref, wc_ref,
                  bg_ref, o_ref):
    x = x_ref[...]
    g0 = _sigmoid(_dot(x, wg0_ref[...]) + bg_ref[0:1, :])
    m = g0 * _dot(a_ref[...], wa_ref[...])
    g1 = _sigmoid(_dot(x, wg1_ref[...]) + bg_ref[1:2, :])
    m = m + g1 * _dot(b_ref[...], wb_ref[...])
    g2 = _sigmoid(_dot(x, wg2_ref[...]) + bg_ref[2:3, :])
    m = m + g2 * _dot(c_ref[...], wc_ref[...])
    o_ref[...] = m.astype(o_ref.dtype)


def _merge(xb, act_a, act_b, act_c, w_in, a_out, b_out, c_out, b_gate, layer):
    t = xb.shape[0]
    tn = TN_MERGE
    resident = dict(pipeline_mode=pl.Buffered(1))
    gspec = lambda k: _col_tile(D_MODEL, tn, lambda i, j: (OFF_G + k * D_MODEL) // tn + j)
    ospec = lambda width: _col_tile(width, tn, lambda i, j: j)
    return pl.pallas_call(
        _merge_kernel,
        grid=(t // TM, D_MODEL // tn),
        in_specs=[
            pl.BlockSpec((TM, D_MODEL), lambda i, j: (i, 0), **resident),
            pl.BlockSpec((TM, D_A), lambda i, j: (i, 0), **resident),
            pl.BlockSpec((TM, D_B), lambda i, j: (i, 0), **resident),
            pl.BlockSpec((TM, D_C), lambda i, j: (i, 0), **resident),
            gspec(0), gspec(1), gspec(2),
            ospec(D_A), ospec(D_B), ospec(D_C),
            pl.BlockSpec((None, N_BRANCH, tn), lambda i, j: (layer, 0, j)),
        ],
        out_specs=pl.BlockSpec((TM, tn), lambda i, j: (i, j)),
        out_shape=jax.ShapeDtypeStruct((t, D_MODEL), BF16),
        compiler_params=_params("parallel", "arbitrary"),
        name="gate_merge",
    )(xb, act_a, act_b, act_c, w_in, w_in, w_in, a_out, b_out, c_out, b_gate)


def _proj_residual_kernel(a_ref, w_ref, x_ref, o_ref, *, alpha):
    half = a_ref.shape[0] // 2
    for rows in (slice(0, half), slice(half, 2 * half)):
        o_ref[rows, :] = alpha * x_ref[rows, :] + _dot(a_ref[rows, :], w_ref[...])


def _proj_residual(a, w, x, alpha, tn, name):
    t, k = a.shape
    return pl.pallas_call(
        functools.partial(_proj_residual_kernel, alpha=alpha),
        grid=(t // TM, D_MODEL // tn),
        in_specs=[pl.BlockSpec((TM, k), lambda i, j: (i, 0),
                               pipeline_mode=pl.Buffered(1) if k > D_MODEL else None),
                  _col_tile(k, tn, lambda i, j: j),
                  pl.BlockSpec((TM, tn), lambda i, j: (i, j))],
        out_specs=pl.BlockSpec((TM, tn), lambda i, j: (i, j)),
        out_shape=jax.ShapeDtypeStruct((t, D_MODEL), F32),
        compiler_params=_params("parallel", "arbitrary"),
        name=name,
    )(a, w, x)


def _ffn_up_kernel(x_ref, w1_ref, w2_ref, cw1_ref, cw2_ref, o_ref, xs_ref, h1_ref, h2_ref,
                   act_ref, carry_ref, *, blocks_per_seq, cast):
    i, j = pl.program_id(0), pl.program_id(1)
    tm, tf = o_ref.shape
    nslab = tf // V7X_LANES
    _stage_block(x_ref, xs_ref, j == 0)
    seq_start = i % blocks_per_seq == 0
    _init_halo(h1_ref.at[:, 0:HALO_SHORT], carry_ref.at[j, 0], seq_start)
    _init_halo(h2_ref.at[:, 0:HALO_SHORT], carry_ref.at[j, 1], seq_start)

    def gate_group(s, q):
        ls = slice(s * V7X_LANES, (s + 1) * V7X_LANES)
        full = (V7X_SUBLANES, V7X_LANES)
        k1 = [jnp.broadcast_to(cw1_ref[k:k + 1, ls], full) for k in range(FF_K)]
        k2 = [jnp.broadcast_to(cw2_ref[k:k + 1, ls], full) for k in range(FF_K)]
        for r in range(ROW_STRIDE):
            row = q * ROW_GROUP + r

            def conv(h_ref, taps):
                at = lambda k: h_ref[s, pl.ds(HALO_SHORT + row - (FF_K - 1) + k, V7X_SUBLANES,
                                              stride=ROW_STRIDE), :]
                return taps[0] * at(0) + taps[1] * at(1) + taps[2] * at(2)

            c1, c2 = conv(h1_ref, k1), conv(h2_ref, k2)
            act_ref[s, pl.ds(row, V7X_SUBLANES, stride=ROW_STRIDE), :] = c1 * _sigmoid(c1) * c2

    assert sum(FFN_PARTS) == tm and all(rows % ROW_GROUP == 0 for rows in FFN_PARTS)
    start = 0
    for r, rows in enumerate(FFN_PARTS):
        x = xs_ref[start:start + rows, :]
        z1, z2 = _dot(x, w1_ref[...]), _dot(x, w2_ref[...])
        data = slice(HALO_SHORT + start, HALO_SHORT + start + rows)
        for s in range(nslab):
            ls = slice(s * V7X_LANES, (s + 1) * V7X_LANES)
            h1_ref[s, data, :] = z1[:, ls]
            h2_ref[s, data, :] = z2[:, ls]
        for s in range(nslab):
            for q in range(start // ROW_GROUP, (start + rows) // ROW_GROUP):
                gate_group(s, q)
        for s in range(nslab):
            ls = slice(s * V7X_LANES, (s + 1) * V7X_LANES)
            o_ref[start:start + rows, ls] = act_ref[s, start:start + rows, :].astype(o_ref.dtype)
        cast(r, len(FFN_PARTS))
        start += rows
    carry_ref[j, 0] = h1_ref[:, tm:tm + HALO_SHORT, :]
    carry_ref[j, 1] = h2_ref[:, tm:tm + HALO_SHORT, :]


def _ffn_up(xb, f_up, f_conv, layer, blocks_per_seq, **casts):
    t = xb.shape[0]
    nj = D_FF // TF
    nslab = TF // V7X_LANES
    (act,), cast_out = _call_with_casts(
        functools.partial(_ffn_up_kernel, blocks_per_seq=blocks_per_seq),
        (xb, f_up, f_up, f_conv, f_conv),
        in_specs=[
            pl.BlockSpec((TM, D_MODEL), lambda i, j: (i, 0)),
            _col_tile(D_MODEL, TF, lambda i, j: j),
            _col_tile(D_MODEL, TF, lambda i, j: nj + j),
            pl.BlockSpec((None, FF_K, TF), lambda i, j: (layer, 0, j)),
            pl.BlockSpec((None, FF_K, TF), lambda i, j: (layer, 0, nj + j)),
        ],
        out_specs=[pl.BlockSpec((TM, TF), lambda i, j: (i, j))],
        out_shapes=[jax.ShapeDtypeStruct((t, D_FF), BF16)],
        grid=(t // TM, nj),
        scratch_shapes=[pltpu.VMEM((TM, D_MODEL), BF16),
                        pltpu.VMEM((nslab, HALO_SHORT + TM, V7X_LANES), F32),
                        pltpu.VMEM((nslab, HALO_SHORT + TM, V7X_LANES), F32),
                        pltpu.VMEM((nslab, TM, V7X_LANES), F32),
                        pltpu.VMEM((nj, 2, nslab, HALO_SHORT, V7X_LANES), F32)],
        semantics=("arbitrary", "arbitrary"),
        name="ffn_up", **casts)
    return act, cast_out


def kernel(x, w_in, b_gate, a_ln_g, a_ln_b, a_ws, a_bs, a_out, b_conv, b_out, c_conv, c_conv_b,
           c_ln_g, c_ln_b, c_out, w_o, ln1_g, ln1_b, f_up, f_conv, f_down, ln2_g, ln2_b):
    bsz, seq, d = x.shape
    depth = w_in.shape[0]
    assert d == D_MODEL and seq % TM == 0 and w_in.shape[-1] == N_IN and f_down.shape[1] == D_FF
    alpha = (2.0 * depth) ** 0.25
    blocks_per_seq = seq // TM
    t = bsz * seq

    weights = (w_in, a_out, b_out, c_out, w_o, f_up, f_down)
    w_in_b = w_in[0].astype(BF16)
    row3 = lambda p: p[:, None, :]
    a_ln_g3, a_ln_b3, c_conv_b3 = map(row3, (a_ln_g, a_ln_b, c_conv_b))
    ln1_g3, ln1_b3, ln2_g3, ln2_b3 = map(row3, (ln1_g, ln1_b, ln2_g, ln2_b))
    slabs = lambda p: p.reshape(depth, D_C // V7X_LANES, 1, V7X_LANES)
    c_ln_g4, c_ln_b4 = slabs(c_ln_g), slabs(c_ln_b)
    bs_wide = jnp.broadcast_to(a_bs[..., None], a_bs.shape + (HD,))

    xf = x.reshape(t, d)
    xb = xf.astype(BF16)
    for l in range(depth):
        if l == 0:
            act_a, (a_out_b, b_out_b, c_out_b, w_o_b) = _sgu_branch(
                xb, w_in_b, a_ln_g3, a_ln_b3, a_ws, bs_wide, l,
                cast_weights=(a_out, b_out, c_out, w_o), cast_layer=l)
            act_b, (f_down_b,) = _sconv_branch(xb, w_in_b, b_conv, l, blocks_per_seq,
                                               cast_weights=(f_down,), cast_layer=l)
            pre_c, (f_up_b,) = _cconv_branch(xb, w_in_b, c_conv, c_conv_b3, l, blocks_per_seq,
                                             cast_weights=(f_up,), cast_layer=l)
        else:
            w_in_b, a_out_b, b_out_b, c_out_b, w_o_b, f_up_b, f_down_b = weights_b
            act_a, _ = _sgu_branch(xb, w_in_b, a_ln_g3, a_ln_b3, a_ws, bs_wide, l)
            act_b, _ = _sconv_branch(xb, w_in_b, b_conv, l, blocks_per_seq)
            pre_c, _ = _cconv_branch(xb, w_in_b, c_conv, c_conv_b3, l, blocks_per_seq)
        act_c = _ln_silu(pre_c, c_ln_g4, c_ln_b4, l)
        m = _merge(xb, act_a, act_b, act_c, w_in_b, a_out_b, b_out_b, c_out_b, b_gate, l)
        mixed = _proj_residual(m, w_o_b, xf, alpha, TN_OUT, "out_proj")
        xf, xb = _ln_rows(mixed, ln1_g3, ln1_b3, l)
        next_weights = weights if l + 1 < depth else ()
        act, weights_b = _ffn_up(xb, f_up_b, f_conv, l, blocks_per_seq,
                                 cast_weights=next_weights, cast_layer=l + 1)
        ffn = _proj_residual(act, f_down_b, xf, alpha, TN_DOWN, "ffn_down")
        xf, xb = _ln_rows(ffn, ln2_g3, ln2_b3, l)
    return xf.reshape(bsz, seq, d)
```

```python
import functools
import math

import jax
import jax.numpy as jnp
from jax.experimental import pallas as pl
from jax.experimental.pallas import tpu as pltpu

F32 = jnp.float32
BF16 = jnp.bfloat16

D_MODEL = 4096
CHUNK = 128
HD = 128
D_A = D_MODEL // 2
H_A = D_A // HD
D_B = D_MODEL // 2
SC_K = 3
D_C = D_MODEL // 2
CF_K = 31
D_FF = 11008
FF_K = 3
N_BRANCH = 3
LN_EPS = 1e-5
OFF_A = 0
OFF_B = OFF_A + 2 * D_A
OFF_C = OFF_B + 3 * D_B
OFF_G = OFF_C + 2 * D_C
N_IN = OFF_G + N_BRANCH * D_MODEL

V7X_SUBLANES = 8
V7X_LANES = 128
BF16_ROWS = 2 * V7X_SUBLANES
V7X_VMEM_BYTES = 64 << 20

TM = 1024
ROWS = 512
FFN_PARTS = (512, 512)
TN_FRONT = 256
TN_MERGE = 256
TN_OUT = 512
TF = 256
TN_DOWN = 256
TR_LN = 256
HALO_SHORT = V7X_SUBLANES
HALO_LONG = 32
ROW_STRIDE = 4
ROW_GROUP = ROW_STRIDE * V7X_SUBLANES
VMEM_LIMIT = V7X_VMEM_BYTES - (8 << 20)

assert ROWS % CHUNK == 0 and TM % ROWS == 0 and ROWS % ROW_GROUP == 0


def _sigmoid(v):
    return 0.5 * (1.0 + jnp.tanh(0.5 * v))


def _gelu_tanh(v):
    c = math.sqrt(2.0 / math.pi)
    return 0.5 * v * (1.0 + jnp.tanh(c * (v + 0.044715 * (v * v * v))))


def _dot(a, b):
    return jnp.dot(a, b, preferred_element_type=F32)


def _params(*sem):
    return pltpu.CompilerParams(dimension_semantics=sem, vmem_limit_bytes=VMEM_LIMIT)


def _col_tile(k, tn, tile_of):
    return pl.BlockSpec((k, tn), lambda *g: (0, tile_of(*g)))


def _cast_jobs(weights, layer, steps):
    in_specs, out_specs, out_shapes = [], [], []
    for job in weights:
        w, c0, ncols, bc = job if isinstance(job, tuple) else (job, 0, job.shape[-1], job.shape[-1])
        r = w.shape[1]
        ncb = ncols // bc
        assert c0 % bc == 0 and ncols % bc == 0
        rows = next(m for m in range(BF16_ROWS, r + 1, BF16_ROWS)
                    if r % m == 0 and (r // m) * ncb <= steps)
        last = (r // rows) * ncb - 1

        def block(step_of, g, ncb=ncb, last=last):
            n = jnp.minimum(step_of(*g), last)
            return n // ncb, n % ncb

        in_specs.append(lambda step_of, rows=rows, bc=bc, c0=c0, block=block: pl.BlockSpec(
            (None, rows, bc),
            lambda *g: (layer, block(step_of, g)[0], c0 // bc + block(step_of, g)[1])))
        out_specs.append(lambda step_of, rows=rows, bc=bc, block=block: pl.BlockSpec(
            (rows, bc), lambda *g: block(step_of, g)))
        out_shapes.append(jax.ShapeDtypeStruct((r, ncols), BF16))
    return in_specs, out_specs, out_shapes


def _split_refs(refs, n_in, n_out, n_cast):
    a, b, c = n_in + n_cast, n_in + n_cast + n_out, n_in + 2 * n_cast + n_out
    return refs[:n_in], refs[a:b], refs[c:], refs[n_in:a], refs[b:c]


def _cast_part(src_refs, dst_refs, part, nparts):
    for src, dst in zip(src_refs, dst_refs):
        c = src.shape[-1] // nparts
        dst[:, part * c:(part + 1) * c] = src[:, part * c:(part + 1) * c].astype(dst.dtype)


def _call_with_casts(body, args, in_specs, out_specs, out_shapes, *, grid, scratch_shapes,
                     semantics, name, cast_weights=(), cast_layer=0):
    n_in, n_out, n_cast = len(in_specs), len(out_specs), len(cast_weights)

    def step_of(*g):
        step = 0
        for index, extent in zip(g, grid):
            step = step * extent + index
        return step

    cast_in, cast_out, cast_shapes = _cast_jobs(cast_weights, cast_layer, math.prod(grid))

    def kernel(*refs):
        ins, outs, scratch, src, dst = _split_refs(refs, n_in, n_out, n_cast)
        body(*ins, *outs, *scratch, cast=functools.partial(_cast_part, src, dst))

    results = pl.pallas_call(
        kernel,
        grid=grid,
        in_specs=list(in_specs) + [spec(step_of) for spec in cast_in],
        out_specs=list(out_specs) + [spec(step_of) for spec in cast_out],
        out_shape=list(out_shapes) + cast_shapes,
        scratch_shapes=scratch_shapes,
        compiler_params=_params(*semantics),
        name=name,
    )(*args, *(job[0] if isinstance(job, tuple) else job for job in cast_weights))
    return results[:n_out], results[n_out:]


def _stage_block(x_ref, xs_ref, first_use):
    @pl.when(first_use)
    def _():
        xs_ref[...] = x_ref[...]


def _init_halo(stage_halo, carry, seq_start):
    @pl.when(seq_start)
    def _():
        stage_halo[...] = jnp.zeros(stage_halo.shape, F32)

    @pl.when(jnp.logical_not(seq_start))
    def _():
        stage_halo[...] = carry[...]


def _conv3(stage, cw, halo, start, rows):
    at = lambda shift: stage[halo + start - shift:halo + start - shift + rows, :]
    return cw[0:1, :] * at(2) + cw[1:2, :] * at(1) + cw[2:3, :] * at(0)


def _sgu_kernel(x_ref, wu_ref, wv_ref, g_ref, b_ref, ws_ref, bs_ref, o_ref, xs_ref, *, cast):
    tm, tn = o_ref.shape
    _stage_block(x_ref, xs_ref, pl.program_id(1) == 0)

    row = jax.lax.broadcasted_iota(jnp.int32, (CHUNK, CHUNK), 0)
    col = jax.lax.broadcasted_iota(jnp.int32, (CHUNK, CHUNK), 1)
    causal = row >= col
    heads = range(tn // HD)
    mix_w = [jnp.where(causal, ws_ref[h], 0.0).astype(BF16) for h in heads]

    def project(p):
        x = xs_ref[p * ROWS:(p + 1) * ROWS, :]
        return _dot(x, wu_ref[...]), _dot(x, wv_ref[...])

    def normalise(zv):
        v = _gelu_tanh(zv)
        out = []
        for h in heads:
            hs = slice(h * HD, (h + 1) * HD)
            vh = v[:, hs]
            mu = jnp.mean(vh, axis=-1, keepdims=True)
            dv = vh - mu
            var = jnp.mean(dv * dv, axis=-1, keepdims=True)
            out.append((dv * jax.lax.rsqrt(var + LN_EPS) * g_ref[:, hs] + b_ref[:, hs]).astype(BF16))
        return out

    def mix(p, zu, vn):
        u = _gelu_tanh(zu)
        for h in heads:
            hs = slice(h * HD, (h + 1) * HD)
            for c in range(ROWS // CHUNK):
                cs = slice(c * CHUNK, (c + 1) * CHUNK)
                mixed = _dot(mix_w[h], vn[h][cs, :]) + bs_ref[h]
                o_ref[p * ROWS + c * CHUNK:p * ROWS + (c + 1) * CHUNK, hs] = (
                    u[cs, hs] * mixed).astype(o_ref.dtype)

    nparts = tm // ROWS
    zu, zv = project(0)
    vn = normalise(zv)
    cast(0, nparts)
    for p in range(1, nparts):
        zu_next, zv_next = project(p)
        mix(p - 1, zu, vn)
        vn = normalise(zv_next)
        zu = zu_next
        cast(p, nparts)
    mix(nparts - 1, zu, vn)


def _sgu_branch(xb, w_in, ln_g, ln_b, ws, bs_wide, layer, **casts):
    t = xb.shape[0]
    tn = TN_FRONT
    (act,), cast_out = _call_with_casts(
        _sgu_kernel, (xb, w_in, w_in, ln_g, ln_b, ws, bs_wide),
        in_specs=[
            pl.BlockSpec((TM, D_MODEL), lambda i, j: (i, 0)),
            _col_tile(D_MODEL, tn, lambda i, j: OFF_A // tn + j),
            _col_tile(D_MODEL, tn, lambda i, j: (OFF_A + D_A) // tn + j),
            pl.BlockSpec((None, 1, tn), lambda i, j: (layer, 0, j)),
            pl.BlockSpec((None, 1, tn), lambda i, j: (layer, 0, j)),
            pl.BlockSpec((None, tn // HD, CHUNK, CHUNK), lambda i, j: (layer, j, 0, 0)),
            pl.BlockSpec((None, tn // HD, CHUNK, HD), lambda i, j: (layer, j, 0, 0)),
        ],
        out_specs=[pl.BlockSpec((TM, tn), lambda i, j: (i, j))],
        out_shapes=[jax.ShapeDtypeStruct((t, D_A), BF16)],
        grid=(t // TM, D_A // tn),
        scratch_shapes=[pltpu.VMEM((TM, D_MODEL), BF16)],
        semantics=("arbitrary", "arbitrary"),
        name="sgu_branch", **casts)
    return act, cast_out


def _sconv_kernel(x_ref, wb_ref, wc_ref, wh_ref, cw_ref, o_ref, xs_ref, p_ref, carry_ref,
                  *, blocks_per_seq, cast):
    i, j = pl.program_id(0), pl.program_id(1)
    tm = o_ref.shape[0]
    _stage_block(x_ref, xs_ref, j == 0)
    _init_halo(p_ref.at[0:HALO_SHORT], carry_ref.at[j], i % blocks_per_seq == 0)

    nparts = tm // ROWS
    for r in range(nparts):
        x = xs_ref[r * ROWS:(r + 1) * ROWS, :]
        p_ref[HALO_SHORT + r * ROWS:HALO_SHORT + (r + 1) * ROWS, :] = (
            _dot(x, wc_ref[...]) * _dot(x, wh_ref[...]))
        y = _conv3(p_ref, cw_ref[...], HALO_SHORT, r * ROWS, ROWS)
        o_ref[r * ROWS:(r + 1) * ROWS, :] = (_dot(x, wb_ref[...]) * y).astype(o_ref.dtype)
        cast(r, nparts)
    carry_ref[j] = p_ref[tm:tm + HALO_SHORT, :]


def _sconv_branch(xb, w_in, b_conv, layer, blocks_per_seq, **casts):
    t = xb.shape[0]
    tn = TN_FRONT
    nj = D_B // tn
    wspec = lambda off: _col_tile(D_MODEL, tn, lambda i, j: off // tn + j)
    (act,), cast_out = _call_with_casts(
        functools.partial(_sconv_kernel, blocks_per_seq=blocks_per_seq),
        (xb, w_in, w_in, w_in, b_conv),
        in_specs=[
            pl.BlockSpec((TM, D_MODEL), lambda i, j: (i, 0)),
            wspec(OFF_B), wspec(OFF_B + D_B), wspec(OFF_B + 2 * D_B),
            pl.BlockSpec((None, SC_K, tn), lambda i, j: (layer, 0, j)),
        ],
        out_specs=[pl.BlockSpec((TM, tn), lambda i, j: (i, j))],
        out_shapes=[jax.ShapeDtypeStruct((t, D_B), BF16)],
        grid=(t // TM, nj),
        scratch_shapes=[pltpu.VMEM((TM, D_MODEL), BF16),
                        pltpu.VMEM((HALO_SHORT + TM, tn), F32),
                        pltpu.VMEM((nj, HALO_SHORT, tn), F32)],
        semantics=("arbitrary", "arbitrary"),
        name="sconv_branch", **casts)
    return act, cast_out


def _cconv_kernel(x_ref, wa_ref, wg_ref, cw_ref, cb_ref, o_ref, xs_ref, ca_ref, carry_ref,
                  *, blocks_per_seq, cast):
    i, j = pl.program_id(0), pl.program_id(1)
    nslab, tm, _ = o_ref.shape
    _stage_block(x_ref, xs_ref, j == 0)
    _init_halo(ca_ref.at[:, 0:HALO_LONG], carry_ref.at[j], i % blocks_per_seq == 0)

    first = HALO_LONG - (CF_K - 1)

    def conv_group(s, q):
        ls = slice(s * V7X_LANES, (s + 1) * V7X_LANES)
        accs = [jnp.broadcast_to(cb_ref[:, ls], (V7X_SUBLANES, V7X_LANES))] * ROW_STRIDE
        for k in range(CF_K):
            wk = jnp.broadcast_to(cw_ref[k:k + 1, ls], (V7X_SUBLANES, V7X_LANES))
            for r in range(ROW_STRIDE):
                rows = pl.ds(q * ROW_GROUP + r + first + k, V7X_SUBLANES, stride=ROW_STRIDE)
                accs[r] = accs[r] + wk * ca_ref[s, rows, :]
        for r in range(ROW_STRIDE):
            o_ref[s, pl.ds(q * ROW_GROUP + r, V7X_SUBLANES, stride=ROW_STRIDE), :] = accs[r]

    groups_per_part = ROWS // ROW_GROUP

    nparts = tm // ROWS
    for part in range(nparts):
        x = xs_ref[part * ROWS:(part + 1) * ROWS, :]
        ca = _dot(x, wa_ref[...]) * _sigmoid(_dot(x, wg_ref[...]))
        data = slice(HALO_LONG + part * ROWS, HALO_LONG + (part + 1) * ROWS)
        for s in range(nslab):
            ca_ref[s, data, :] = ca[:, s * V7X_LANES:(s + 1) * V7X_LANES]
        cast(part, nparts)
        for s in range(nslab):
            for q in range(part * groups_per_part, (part + 1) * groups_per_part):
                conv_group(s, q)
    carry_ref[j] = ca_ref[:, tm:tm + HALO_LONG, :]


def _cconv_branch(xb, w_in, c_conv, c_conv_b, layer, blocks_per_seq, **casts):
    t = xb.shape[0]
    tn = TN_FRONT
    nj = D_C // tn
    nslab = tn // V7X_LANES
    wspec = lambda off: _col_tile(D_MODEL, tn, lambda i, j: off // tn + j)
    (pre,), cast_out = _call_with_casts(
        functools.partial(_cconv_kernel, blocks_per_seq=blocks_per_seq),
        (xb, w_in, w_in, c_conv, c_conv_b),
        in_specs=[
            pl.BlockSpec((TM, D_MODEL), lambda i, j: (i, 0)),
            wspec(OFF_C), wspec(OFF_C + D_C),
            pl.BlockSpec((None, CF_K, tn), lambda i, j: (layer, 0, j)),
            pl.BlockSpec((None, 1, tn), lambda i, j: (layer, 0, j)),
        ],
        out_specs=[pl.BlockSpec((nslab, TM, V7X_LANES), lambda i, j: (j, i, 0))],
        out_shapes=[jax.ShapeDtypeStruct((D_C // V7X_LANES, t, V7X_LANES), F32)],
        grid=(t // TM, nj),
        scratch_shapes=[pltpu.VMEM((TM, D_MODEL), BF16),
                        pltpu.VMEM((nslab, HALO_LONG + TM, V7X_LANES), F32),
                        pltpu.VMEM((nj, nslab, HALO_LONG, V7X_LANES), F32)],
        semantics=("arbitrary", "arbitrary"),
        name="cconv_branch", **casts)
    return pre, cast_out


def _ln_silu_kernel(v_ref, g_ref, b_ref, o_ref):
    nslab, _, lanes = v_ref.shape
    width = nslab * lanes
    v = v_ref[...]
    mu = jnp.sum(jnp.sum(v, axis=0), axis=-1, keepdims=True) / width
    dv = v - mu[None]
    var = jnp.sum(jnp.sum(dv * dv, axis=0), axis=-1, keepdims=True) / width
    y = dv * jax.lax.rsqrt(var + LN_EPS)[None] * g_ref[...] + b_ref[...]
    y = y * _sigmoid(y)
    for s in range(nslab):
        o_ref[:, s * lanes:(s + 1) * lanes] = y[s].astype(o_ref.dtype)


def _ln_silu(v, g, b, layer):
    nslab, t, lanes = v.shape
    par_spec = pl.BlockSpec((None, nslab, 1, lanes), lambda i: (layer, 0, 0, 0))
    return pl.pallas_call(
        _ln_silu_kernel,
        grid=(t // TR_LN,),
        in_specs=[pl.BlockSpec((nslab, TR_LN, lanes), lambda i: (0, i, 0)), par_spec, par_spec],
        out_specs=pl.BlockSpec((TR_LN, nslab * lanes), lambda i: (i, 0)),
        out_shape=jax.ShapeDtypeStruct((t, nslab * lanes), BF16),
        compiler_params=_params("parallel"),
        name="ln_silu",
    )(v, g, b)


def _ln_rows_kernel(v_ref, g_ref, b_ref, o_ref, ob_ref):
    v = v_ref[...]
    mu = jnp.mean(v, axis=-1, keepdims=True)
    dv = v - mu
    var = jnp.mean(dv * dv, axis=-1, keepdims=True)
    y = dv * jax.lax.rsqrt(var + LN_EPS) * g_ref[...] + b_ref[...]
    o_ref[...] = y
    ob_ref[...] = y.astype(ob_ref.dtype)


def _ln_rows(v, g, b, layer):
    t, c = v.shape
    row_spec = pl.BlockSpec((TR_LN, c), lambda i: (i, 0))
    par_spec = pl.BlockSpec((None, 1, c), lambda i: (layer, 0, 0))
    return pl.pallas_call(
        _ln_rows_kernel,
        grid=(t // TR_LN,),
        in_specs=[row_spec, par_spec, par_spec],
        out_specs=[row_spec, row_spec],
        out_shape=[jax.ShapeDtypeStruct((t, c), F32), jax.ShapeDtypeStruct((t, c), BF16)],
        compiler_params=_params("parallel"),
        name="ln_rows",
    )(v, g, b)


def _merge_kernel(x_ref, a_ref, b_ref, c_ref, wg0_ref, wg1_ref, wg2_ref, wa_ref, wb_ref, wc_ref,
                  bg_ref, o_ref):
    x = x_ref[...]
    g0 = _sigmoid(_dot(x, wg0_ref[...]) + bg_ref[0:1, :])
    m = g0 * _dot(a_ref[...], wa_ref[...])
    g1 = _sigmoid(_dot(x, wg1_ref[...]) + bg_ref[1:2, :])
    m = m + g1 * _dot(b_ref[...], wb_ref[...])
    g2 = _sigmoid(_dot(x, wg2_ref[...]) + bg_ref[2:3, :])
    m = m + g2 * _dot(c_ref[...], wc_ref[...])
    o_ref[...] = m.astype(o_ref.dtype)


def _merge(xb, act_a, act_b, act_c, w_gate, a_out, b_out, c_out, b_gate, layer):
    t = xb.shape[0]
    tn = TN_MERGE
    resident = dict(pipeline_mode=pl.Buffered(1))
    gate0 = w_gate.shape[-1] - N_BRANCH * D_MODEL
    gspec = lambda k: _col_tile(D_MODEL, tn, lambda i, j: (gate0 + k * D_MODEL) // tn + j)
    ospec = lambda width: _col_tile(width, tn, lambda i, j: j)
    return pl.pallas_call(
        _merge_kernel,
        grid=(t // TM, D_MODEL // tn),
        in_specs=[
            pl.BlockSpec((TM, D_MODEL), lambda i, j: (i, 0), **resident),
            pl.BlockSpec((TM, D_A), lambda i, j: (i, 0), **resident),
            pl.BlockSpec((TM, D_B), lambda i, j: (i, 0), **resident),
            pl.BlockSpec((TM, D_C), lambda i, j: (i, 0), **resident),
            gspec(0), gspec(1), gspec(2),
            ospec(D_A), ospec(D_B), ospec(D_C),
            pl.BlockSpec((None, N_BRANCH, tn), lambda i, j: (layer, 0, j)),
        ],
        out_specs=pl.BlockSpec((TM, tn), lambda i, j: (i, j)),
        out_shape=jax.ShapeDtypeStruct((t, D_MODEL), BF16),
        compiler_params=_params("parallel", "arbitrary"),
        name="gate_merge",
    )(xb, act_a, act_b, act_c, w_gate, w_gate, w_gate, a_out, b_out, c_out, b_gate)


def _proj_residual_kernel(a_ref, w_ref, x_ref, o_ref, *, alpha):
    half = a_ref.shape[0] // 2
    for rows in (slice(0, half), slice(half, 2 * half)):
        o_ref[rows, :] = alpha * x_ref[rows, :] + _dot(a_ref[rows, :], w_ref[...])


def _proj_residual(a, w, x, alpha, tn, name):
    t, k = a.shape
    return pl.pallas_call(
        functools.partial(_proj_residual_kernel, alpha=alpha),
        grid=(t // TM, D_MODEL // tn),
        in_specs=[pl.BlockSpec((TM, k), lambda i, j: (i, 0),
                               pipeline_mode=pl.Buffered(1) if k > D_MODEL else None),
                  _col_tile(k, tn, lambda i, j: j),
                  pl.BlockSpec((TM, tn), lambda i, j: (i, j))],
        out_specs=pl.BlockSpec((TM, tn), lambda i, j: (i, j)),
        out_shape=jax.ShapeDtypeStruct((t, D_MODEL), F32),
        compiler_params=_params("parallel", "arbitrary"),
        name=name,
    )(a, w, x)


def _ffn_up_kernel(x_ref, w1_ref, w2_ref, cw1_ref, cw2_ref, o_ref, xs_ref, h1_ref, h2_ref,
                   act_ref, carry_ref, *, blocks_per_seq, cast):
    i, j = pl.program_id(0), pl.program_id(1)
    tm, tf = o_ref.shape
    nslab = tf // V7X_LANES
    _stage_block(x_ref, xs_ref, j == 0)
    seq_start = i % blocks_per_seq == 0
    _init_halo(h1_ref.at[:, 0:HALO_SHORT], carry_ref.at[j, 0], seq_start)
    _init_halo(h2_ref.at[:, 0:HALO_SHORT], carry_ref.at[j, 1], seq_start)

    def gate_group(s, q):
        ls = slice(s * V7X_LANES, (s + 1) * V7X_LANES)
        full = (V7X_SUBLANES, V7X_LANES)
        k1 = [jnp.broadcast_to(cw1_ref[k:k + 1, ls], full) for k in range(FF_K)]
        k2 = [jnp.broadcast_to(cw2_ref[k:k + 1, ls], full) for k in range(FF_K)]
        for r in range(ROW_STRIDE):
            row = q * ROW_GROUP + r

            def conv(h_ref, taps):
                at = lambda k: h_ref[s, pl.ds(HALO_SHORT + row - (FF_K - 1) + k, V7X_SUBLANES,
                                              stride=ROW_STRIDE), :]
                return taps[0] * at(0) + taps[1] * at(1) + taps[2] * at(2)

            c1, c2 = conv(h1_ref, k1), conv(h2_ref, k2)
            act_ref[s, pl.ds(row, V7X_SUBLANES, stride=ROW_STRIDE), :] = c1 * _sigmoid(c1) * c2

    assert sum(FFN_PARTS) == tm and all(rows % ROW_GROUP == 0 for rows in FFN_PARTS)
    start = 0
    for r, rows in enumerate(FFN_PARTS):
        x = xs_ref[start:start + rows, :]
        z1, z2 = _dot(x, w1_ref[...]), _dot(x, w2_ref[...])
        data = slice(HALO_SHORT + start, HALO_SHORT + start + rows)
        for s in range(nslab):
            ls = slice(s * V7X_LANES, (s + 1) * V7X_LANES)
            h1_ref[s, data, :] = z1[:, ls]
            h2_ref[s, data, :] = z2[:, ls]
        for s in range(nslab):
            for q in range(start // ROW_GROUP, (start + rows) // ROW_GROUP):
                gate_group(s, q)
        for s in range(nslab):
            ls = slice(s * V7X_LANES, (s + 1) * V7X_LANES)
            o_ref[start:start + rows, ls] = act_ref[s, start:start + rows, :].astype(o_ref.dtype)
        cast(r, len(FFN_PARTS))
        start += rows
    carry_ref[j, 0] = h1_ref[:, tm:tm + HALO_SHORT, :]
    carry_ref[j, 1] = h2_ref[:, tm:tm + HALO_SHORT, :]


def _ffn_up(xb, f_up, f_conv, layer, blocks_per_seq, **casts):
    t = xb.shape[0]
    nj = D_FF // TF
    nslab = TF // V7X_LANES
    (act,), cast_out = _call_with_casts(
        functools.partial(_ffn_up_kernel, blocks_per_seq=blocks_per_seq),
        (xb, f_up, f_up, f_conv, f_conv),
        in_specs=[
            pl.BlockSpec((TM, D_MODEL), lambda i, j: (i, 0)),
            _col_tile(D_MODEL, TF, lambda i, j: j),
            _col_tile(D_MODEL, TF, lambda i, j: nj + j),
            pl.BlockSpec((None, FF_K, TF), lambda i, j: (layer, 0, j)),
            pl.BlockSpec((None, FF_K, TF), lambda i, j: (layer, 0, nj + j)),
        ],
        out_specs=[pl.BlockSpec((TM, TF), lambda i, j: (i, j))],
        out_shapes=[jax.ShapeDtypeStruct((t, D_FF), BF16)],
        grid=(t // TM, nj),
        scratch_shapes=[pltpu.VMEM((TM, D_MODEL), BF16),
                        pltpu.VMEM((nslab, HALO_SHORT + TM, V7X_LANES), F32),
                        pltpu.VMEM((nslab, HALO_SHORT + TM, V7X_LANES), F32),
                        pltpu.VMEM((nslab, TM, V7X_LANES), F32),
                        pltpu.VMEM((nj, 2, nslab, HALO_SHORT, V7X_LANES), F32)],
        semantics=("arbitrary", "arbitrary"),
        name="ffn_up", **casts)
    return act, cast_out


def kernel(x, w_in, b_gate, a_ln_g, a_ln_b, a_ws, a_bs, a_out, b_conv, b_out, c_conv, c_conv_b,
           c_ln_g, c_ln_b, c_out, w_o, ln1_g, ln1_b, f_up, f_conv, f_down, ln2_g, ln2_b):
    bsz, seq, d = x.shape
    depth = w_in.shape[0]
    assert d == D_MODEL and seq % TM == 0 and w_in.shape[-1] == N_IN and f_down.shape[1] == D_FF
    alpha = (2.0 * depth) ** 0.25
    blocks_per_seq = seq // TM
    t = bsz * seq

    weights = (w_in, a_out, b_out, c_out, w_o, f_up, f_down)
    w_in_b = w_in[0, :, :OFF_G].astype(BF16)
    gate_cols = (w_in, OFF_G, N_IN - OFF_G, math.gcd(OFF_G, N_IN - OFF_G))
    row3 = lambda p: p[:, None, :]
    a_ln_g3, a_ln_b3, c_conv_b3 = map(row3, (a_ln_g, a_ln_b, c_conv_b))
    ln1_g3, ln1_b3, ln2_g3, ln2_b3 = map(row3, (ln1_g, ln1_b, ln2_g, ln2_b))
    slabs = lambda p: p.reshape(depth, D_C // V7X_LANES, 1, V7X_LANES)
    c_ln_g4, c_ln_b4 = slabs(c_ln_g), slabs(c_ln_b)
    bs_wide = jnp.broadcast_to(a_bs[..., None], a_bs.shape + (HD,))

    xf = x.reshape(t, d)
    xb = xf.astype(BF16)
    for l in range(depth):
        if l == 0:
            act_a, (a_out_b, b_out_b, c_out_b, w_o_b, w_gate_b) = _sgu_branch(
                xb, w_in_b, a_ln_g3, a_ln_b3, a_ws, bs_wide, l,
                cast_weights=(a_out, b_out, c_out, w_o, gate_cols), cast_layer=l)
            act_b, (f_down_b,) = _sconv_branch(xb, w_in_b, b_conv, l, blocks_per_seq,
                                               cast_weights=(f_down,), cast_layer=l)
            pre_c, (f_up_b,) = _cconv_branch(xb, w_in_b, c_conv, c_conv_b3, l, blocks_per_seq,
                                             cast_weights=(f_up,), cast_layer=l)
        else:
            w_in_b, a_out_b, b_out_b, c_out_b, w_o_b, f_up_b, f_down_b = weights_b
            w_gate_b = w_in_b
            act_a, _ = _sgu_branch(xb, w_in_b, a_ln_g3, a_ln_b3, a_ws, bs_wide, l)
            act_b, _ = _sconv_branch(xb, w_in_b, b_conv, l, blocks_per_seq)
            pre_c, _ = _cconv_branch(xb, w_in_b, c_conv, c_conv_b3, l, blocks_per_seq)
        act_c = _ln_silu(pre_c, c_ln_g4, c_ln_b4, l)
        m = _merge(xb, act_a, act_b, act_c, w_gate_b, a_out_b, b_out_b, c_out_b, b_gate, l)
        mixed = _proj_residual(m, w_o_b, xf, alpha, TN_OUT, "out_proj")
        xf, xb = _ln_rows(mixed, ln1_g3, ln1_b3, l)
        next_weights = weights if l + 1 < depth else ()
        act, weights_b = _ffn_up(xb, f_up_b, f_conv, l, blocks_per_seq,
                                 cast_weights=next_weights, cast_layer=l + 1)
        ffn = _proj_residual(act, f_down_b, xf, alpha, TN_DOWN, "ffn_down")
        xf, xb = _ln_rows(ffn, ln2_g3, ln2_b3, l)
    return xf.reshape(bsz, seq, d)
```
